```python
import math
import jax, jax.numpy as jnp
from jax import lax
import numpy as np

D_MODEL = 1024
BATCH = 16
SEQ = 2048
DEPTH = 2

N_MEM = 256
EPS = 1e-6
MIX_WIDTH = D_MODEL
M_HEADS = 4
M_WIDTH = MIX_WIDTH // 2
M_DV = M_WIDTH // M_HEADS
M_DK = M_DV // 2
CONV_K = 4
CHUNK = 64
F_BIAS_LO = 3.0
F_BIAS_HI = 6.0
A_HEADS = 4
A_DV = 128
A_WIDTH = A_HEADS * A_DV
Q_LORA = 256
KV_LORA = 128
NOPE_DIM = 128
ROPE_DIM = 64
A_QK = NOPE_DIM + ROPE_DIM
ROPE_THETA = 10000.0
Q_BLOCK = 128
IN_SPLITS = (M_WIDTH, M_WIDTH, M_WIDTH, M_HEADS, M_HEADS, Q_LORA, KV_LORA, ROPE_DIM)
IN_COLS = sum(IN_SPLITS)
X_HEADS = 4
X_HD = 128
X_WIDTH = X_HEADS * X_HD
D_FF = 4 * D_MODEL

kernel_name = "hybrid_mlstm_mla_memxattn_block"


def rmsnorm(x, g):
    xf = x.astype(jnp.float32)
    y = xf * lax.rsqrt(jnp.mean(xf * xf, axis=-1, keepdims=True) + EPS)
    return (y * g.astype(jnp.float32)).astype(x.dtype)


def rope_tables(positions):
    inv = 1.0 / (ROPE_THETA ** (jnp.arange(0, ROPE_DIM, 2, dtype=jnp.float32) / ROPE_DIM))
    ang = positions.astype(jnp.float32)[..., None] * inv
    return jnp.cos(ang), jnp.sin(ang)


def apply_rope(x, cos, sin):
    extra = x.ndim - 3
    shp = cos.shape[:2] + (1,) * extra + cos.shape[-1:]
    c = cos.reshape(shp)
    s = sin.reshape(shp)
    x1, x2 = jnp.split(x.astype(jnp.float32), 2, axis=-1)
    return jnp.concatenate([x1 * c - x2 * s, x2 * c + x1 * s], axis=-1).astype(x.dtype)


def causal_dwconv(u, w, b):
    C = u.shape[-1]
    y = lax.conv_general_dilated(
        u, w[:, None, :].astype(u.dtype), window_strides=(1,),
        padding=[(CONV_K - 1, 0)], dimension_numbers=('NWC', 'WIO', 'NWC'),
        feature_group_count=C)
    return y + b


def mlstm_chunkwise(q, k, v, ig, fg):
    f32 = jnp.float32
    Bn, H, S, DK = q.shape
    DV = v.shape[-1]
    L = CHUNK
    NC = S // L
    qc = (q.astype(f32) * (DK ** -0.5)).reshape(Bn, H, NC, L, DK)
    kc = k.astype(f32).reshape(Bn, H, NC, L, DK)
    vc = v.astype(f32).reshape(Bn, H, NC, L, DV)
    logf = jax.nn.log_sigmoid(fg.astype(f32)).reshape(Bn, H, NC, L)
    ic = ig.astype(f32).reshape(Bn, H, NC, L)
    b = jnp.cumsum(logf, axis=-1)
    b_tot = b[..., -1]

    a = b_tot[..., None] - b + ic
    m_loc = jnp.max(a, axis=-1)
    wa = jnp.exp(a - m_loc[..., None])
    C_loc = jnp.einsum('bhcl,bhcld,bhcle->bhcde', wa, kc, vc)
    n_loc = jnp.einsum('bhcl,bhcld->bhcd', wa, kc)

    def step(carry, inp):
        C, n, m = carry
        bt, ml, Cl, nl = inp
        m_new = jnp.maximum(bt + m, ml)
        s_old = jnp.exp(bt + m - m_new)
        s_loc = jnp.exp(ml - m_new)
        C_new = s_old[..., None, None] * C + s_loc[..., None, None] * Cl
        n_new = s_old[..., None] * n + s_loc[..., None] * nl
        return (C_new, n_new, m_new), (C, n, m)

    init = (jnp.zeros((Bn, H, DK, DV), f32), jnp.zeros((Bn, H, DK), f32), jnp.zeros((Bn, H), f32))
    xs = (jnp.moveaxis(b_tot, 2, 0), jnp.moveaxis(m_loc, 2, 0),
          jnp.moveaxis(C_loc, 2, 0), jnp.moveaxis(n_loc, 2, 0))
    _, (C_prev, n_prev, m_prev) = lax.scan(step, init, xs)
    C_prev = jnp.moveaxis(C_prev, 0, 2)
    n_prev = jnp.moveaxis(n_prev, 0, 2)
    m_prev = jnp.moveaxis(m_prev, 0, 2)

    causal = jnp.tril(jnp.ones((L, L), dtype=bool))
    Dlog = b[..., :, None] - b[..., None, :] + ic[..., None, :]
    Dlog = jnp.where(causal, Dlog, -jnp.inf)
    inter_log = b + m_prev[..., None]
    m_j = jnp.maximum(inter_log, jnp.max(Dlog, axis=-1))
    Dw = jnp.exp(Dlog - m_j[..., None])
    qk = jnp.einsum('bhcjd,bhcsd->bhcjs', qc, kc) * Dw
    s_inter = jnp.exp(inter_log - m_j)
    num = (s_inter[..., None] * jnp.einsum('bhcjd,bhcde->bhcje', qc, C_prev)
           + jnp.einsum('bhcjs,bhcse->bhcje', qk, vc))
    den = s_inter * jnp.einsum('bhcjd,bhcd->bhcj', qc, n_prev) + jnp.sum(qk, axis=-1)
    h = num / jnp.maximum(jnp.abs(den), jnp.exp(-m_j))[..., None]
    return h.reshape(Bn, H, S, DV).astype(v.dtype)


def causal_block_attention(q, k, v):
    Bn, H, S, Dq = q.shape
    Dv = v.shape[-1]
    nb = S // Q_BLOCK
    scale = Dq ** -0.5
    qb = q.reshape(Bn, H, nb, Q_BLOCK, Dq).transpose(2, 0, 1, 3, 4)
    kpos = jnp.arange(S)

    def one_block(args):
        qi, bi = args
        s = jnp.einsum('bhqd,bhkd->bhqk', qi, k).astype(jnp.float32) * scale
        qpos = bi * Q_BLOCK + jnp.arange(Q_BLOCK)
        s = jnp.where(kpos[None, :] <= qpos[:, None], s, -jnp.inf)
        p = jax.nn.softmax(s, axis=-1)
        return jnp.einsum('bhqk,bhkd->bhqd', p.astype(v.dtype), v)

    out = lax.map(one_block, (qb, jnp.arange(nb)))
    return out.transpose(1, 2, 0, 3, 4).reshape(Bn, H, S, Dv)


def parallel_mixer(h, cos, sin, w_in, conv_w, conv_b, wq_m, wk_m, b_igate, b_fgate, m_out_g,
                   cq_norm_g, ckv_norm_g, w_uq, w_ukv, qk_norm_q, qk_norm_k, a_out_g, w_out):
    Bn, S, _ = h.shape
    proj = h @ w_in
    idx = np.cumsum(IN_SPLITS)[:-1].tolist()
    u, v_m, o_pre, i_pre, f_pre, c_q, c_kv, k_r = jnp.split(proj, idx, axis=-1)

    u_c = jax.nn.silu(causal_dwconv(u, conv_w, conv_b))
    u_h = u_c.reshape(Bn, S, M_HEADS, M_DV)
    q_m = jnp.einsum('bshd,hde->bhse', u_h, wq_m)
    k_m = jnp.einsum('bshd,hde->bhse', u_h, wk_m)
    v_h = v_m.reshape(Bn, S, M_HEADS, M_DV).transpose(0, 2, 1, 3)
    ig = (i_pre + b_igate).transpose(0, 2, 1)
    fg = (f_pre + b_fgate).transpose(0, 2, 1)
    hm = mlstm_chunkwise(q_m, k_m, v_h, ig, fg)
    hm = rmsnorm(hm.transpose(0, 2, 1, 3), m_out_g)
    y_m = jax.nn.sigmoid(o_pre) * hm.reshape(Bn, S, M_WIDTH)

    q_a = (rmsnorm(c_q, cq_norm_g) @ w_uq).reshape(Bn, S, A_HEADS, A_QK)
    kv = (rmsnorm(c_kv, ckv_norm_g) @ w_ukv).reshape(Bn, S, A_HEADS, NOPE_DIM + A_DV)
    k_nope, v_a = jnp.split(kv, [NOPE_DIM], axis=-1)
    q_nope = rmsnorm(q_a[..., :NOPE_DIM], qk_norm_q[:NOPE_DIM])
    q_rope = apply_rope(rmsnorm(q_a[..., NOPE_DIM:], qk_norm_q[NOPE_DIM:]), cos, sin)
    k_nope = rmsnorm(k_nope, qk_norm_k[:NOPE_DIM])
    k_rope = apply_rope(rmsnorm(k_r, qk_norm_k[NOPE_DIM:]), cos, sin)
    k_rope = jnp.broadcast_to(k_rope[:, :, None, :], (Bn, S, A_HEADS, ROPE_DIM))
    q_full = jnp.concatenate([q_nope, q_rope], axis=-1).transpose(0, 2, 1, 3)
    k_full = jnp.concatenate([k_nope, k_rope], axis=-1).transpose(0, 2, 1, 3)
    o_a = causal_block_attention(q_full, k_full, v_a.transpose(0, 2, 1, 3))
    y_a = rmsnorm(o_a.transpose(0, 2, 1, 3), a_out_g).reshape(Bn, S, A_WIDTH)

    return jnp.concatenate([y_m, y_a], axis=-1) @ w_out


def memory_cross_attention(h, mem_n, wq_x, wkv_x, xq_norm_g, xk_norm_g, wo_x):
    Bn, S, _ = h.shape
    Nm = mem_n.shape[1]
    q = rmsnorm((h @ wq_x).reshape(Bn, S, X_HEADS, X_HD), xq_norm_g)
    kv = (mem_n @ wkv_x).reshape(Bn, Nm, 2, X_HEADS, X_HD)
    k = rmsnorm(kv[:, :, 0], xk_norm_g)
    v = kv[:, :, 1]
    s = jnp.einsum('bshd,bnhd->bhsn', q, k).astype(jnp.float32) * (X_HD ** -0.5)
    p = jax.nn.softmax(s, axis=-1)
    o = jnp.einsum('bhsn,bnhd->bshd', p.astype(v.dtype), v).reshape(Bn, S, X_WIDTH)
    return o @ wo_x


def setup_inputs(seed: int = 0) -> dict:
    key = jax.random.key(seed)
    ks = jax.random.split(key, 32)
    f32 = jnp.float32
    nrm = lambda k, shp, scale: jax.random.normal(k, shp, f32) * scale
    gain = lambda k, shp: 1.0 + 0.02 * jax.random.normal(k, shp, f32)
    Ld = DEPTH
    x = jax.random.normal(ks[0], (BATCH, SEQ, D_MODEL), f32)
    mem = jax.random.normal(ks[1], (BATCH, N_MEM, D_MODEL), f32)
    offset = jax.random.randint(ks[2], (BATCH, 1), 0, 1024, dtype=jnp.int32)
    positions = (offset + jnp.arange(SEQ, dtype=jnp.int32)[None, :]).astype(jnp.int32)
    b_f = (jnp.linspace(F_BIAS_LO, F_BIAS_HI, M_HEADS, dtype=f32)[None, :]
           + 0.1 * jax.random.normal(ks[10], (Ld, M_HEADS), f32))
    return {
        "x": x,
        "mem": mem,
        "positions": positions,
        "norm_mix_g": gain(ks[3], (Ld, D_MODEL)),
        "w_in": nrm(ks[4], (Ld, D_MODEL, IN_COLS), D_MODEL ** -0.5),
        "conv_w": nrm(ks[5], (Ld, CONV_K, M_WIDTH), CONV_K ** -0.5),
        "conv_b": nrm(ks[6], (Ld, M_WIDTH), 0.02),
        "wq_m": nrm(ks[7], (Ld, M_HEADS, M_DV, M_DK), M_DV ** -0.5),
        "wk_m": nrm(ks[8], (Ld, M_HEADS, M_DV, M_DK), M_DV ** -0.5),
        "b_igate": nrm(ks[9], (Ld, M_HEADS), 0.1),
        "b_fgate": b_f,
        "m_out_g": gain(ks[11], (Ld, M_HEADS, M_DV)),
        "cq_norm_g": gain(ks[12], (Ld, Q_LORA)),
        "ckv_norm_g": gain(ks[13], (Ld, KV_LORA)),
        "w_uq": nrm(ks[14], (Ld, Q_LORA, A_HEADS * A_QK), Q_LORA ** -0.5),
        "w_ukv": nrm(ks[15], (Ld, KV_LORA, A_HEADS * (NOPE_DIM + A_DV)), KV_LORA ** -0.5),
        "qk_norm_q": gain(ks[16], (Ld, A_QK)),
        "qk_norm_k": gain(ks[17], (Ld, A_QK)),
        "a_out_g": gain(ks[18], (Ld, A_HEADS, A_DV)),
        "w_out": nrm(ks[19], (Ld, MIX_WIDTH, D_MODEL), MIX_WIDTH ** -0.5),
        "norm_x_g": gain(ks[20], (Ld, D_MODEL)),
        "norm_mem_g": gain(ks[21], (Ld, D_MODEL)),
        "wq_x": nrm(ks[22], (Ld, D_MODEL, X_WIDTH), D_MODEL ** -0.5),
        "wkv_x": nrm(ks[23], (Ld, D_MODEL, 2 * X_WIDTH), D_MODEL ** -0.5),
        "xq_norm_g": gain(ks[24], (Ld, X_HD)),
        "xk_norm_g": gain(ks[25], (Ld, X_HD)),
        "wo_x": nrm(ks[26], (Ld, X_WIDTH, D_MODEL), X_WIDTH ** -0.5),
        "norm_ffn_g": gain(ks[27], (Ld, D_MODEL)),
        "w_ff1": nrm(ks[28], (Ld, D_MODEL, D_FF), D_MODEL ** -0.5),
        "w_ff2": nrm(ks[29], (Ld, D_FF, D_MODEL), D_FF ** -0.5),
    }


def reference(x, mem, positions, norm_mix_g, w_in, conv_w, conv_b, wq_m, wk_m, b_igate, b_fgate,
              m_out_g, cq_norm_g, ckv_norm_g, w_uq, w_ukv, qk_norm_q, qk_norm_k, a_out_g, w_out,
              norm_x_g, norm_mem_g, wq_x, wkv_x, xq_norm_g, xk_norm_g, wo_x,
              norm_ffn_g, w_ff1, w_ff2):
    cos, sin = rope_tables(positions)
    for l in range(DEPTH):
        h = rmsnorm(x, norm_mix_g[l])
        x = x + parallel_mixer(h, cos, sin, w_in[l], conv_w[l], conv_b[l], wq_m[l], wk_m[l],
                               b_igate[l], b_fgate[l], m_out_g[l], cq_norm_g[l], ckv_norm_g[l],
                               w_uq[l], w_ukv[l], qk_norm_q[l], qk_norm_k[l], a_out_g[l], w_out[l])
        h = rmsnorm(x, norm_x_g[l])
        mem_n = rmsnorm(mem, norm_mem_g[l])
        x = x + memory_cross_attention(h, mem_n, wq_x[l], wkv_x[l], xq_norm_g[l], xk_norm_g[l], wo_x[l])
        h = rmsnorm(x, norm_ffn_g[l])
        x = x + jnp.square(jax.nn.relu(h @ w_ff1[l])) @ w_ff2[l]
    return x
```

```python
import functools

import jax
import jax.numpy as jnp
from jax import lax
from jax.experimental import pallas as pl
from jax.experimental.pallas import tpu as pltpu

f32 = jnp.float32
bf16 = jnp.bfloat16

EPS = 1e-6
D_MODEL = 1024
M_HEADS = 4
M_WIDTH = 512
M_DV = 128
M_DK = 64
CONV_K = 4
CHUNK = 64
A_HEADS = 4
A_DV = 128
Q_LORA = 256
KV_LORA = 128
NOPE_DIM = 128
ROPE_DIM = 64
A_QK = NOPE_DIM + ROPE_DIM
ROPE_THETA = 10000.0
X_HEADS = 4
X_HD = 128
X_WIDTH = 512
D_FF = 4096
FF_CHUNK = 1024

QK_PAD = 256
C_U, C_V, C_O, C_G, C_CQ, C_CKV, C_KR, C_END = 0, 512, 1024, 1536, 1664, 1920, 2048, 2176

VMEM_LIMIT = 56 * 1024 * 1024


def _dot(a, b):
    return jnp.dot(a, b, preferred_element_type=f32)


def _dot_nt(a, b):
    return lax.dot_general(a, b, (((1,), (1,)), ((), ())), preferred_element_type=f32)


def _rms(x, g, n=None):
    n = x.shape[-1] if n is None else n
    ms = jnp.sum(x * x, axis=-1, keepdims=True) * (1.0 / n)
    return x * lax.rsqrt(ms + EPS) * g


def _sigmoid(x):
    return 1.0 / (1.0 + jnp.exp(-x))


def _log_sigmoid(x):
    return jnp.minimum(x, 0.0) - jnp.log1p(jnp.exp(-jnp.abs(x)))


def _rope_kernel(pos_ref, inv_ref, sgn_ref, cos_ref, sin_ref):
    ang = pos_ref[...].astype(f32) * inv_ref[...]
    cos_ref[...] = jnp.cos(ang) * jnp.abs(sgn_ref[...])
    sin_ref[...] = jnp.sin(ang) * sgn_ref[...]


def _rope_tables(positions):
    T = positions.size
    tm = min(T, 2048)
    inv = 1.0 / (ROPE_THETA ** (jnp.arange(0, ROPE_DIM, 2, dtype=f32) / ROPE_DIM))
    z = jnp.zeros((ROPE_DIM // 2,), f32)
    o = jnp.ones((ROPE_DIM // 2,), f32)
    inv_p = jnp.concatenate([inv, z, inv, z])[None, :]
    sgn = jnp.concatenate([-o, z, o, z])[None, :]
    return pl.pallas_call(
        _rope_kernel,
        grid=(T // tm,),
        in_specs=[pl.BlockSpec((tm, 1), lambda i: (i, 0)),
                  pl.BlockSpec((1, 128), lambda i: (0, 0)),
                  pl.BlockSpec((1, 128), lambda i: (0, 0))],
        out_specs=[pl.BlockSpec((tm, 128), lambda i: (i, 0)),
                   pl.BlockSpec((tm, 128), lambda i: (i, 0))],
        out_shape=[jax.ShapeDtypeStruct((T, 128), f32)] * 2,
        name="rope_tables",
    )(positions.reshape(T, 1), inv_p, sgn)


def _front_kernel(x_ref, cos_ref, sin_ref, gmix_ref, win_ref, convw_ref, convb_ref,
                  wq_ref, wk_ref, wkt_ref, gbias_ref, mog_ref, cqg_ref, ckvg_ref,
                  wuq_ref, wukv_ref, gq_ref, gk_ref,
                  ym_ref, q_ref, k_ref, v_ref,
                  proj_ref, ubuf_ref, cst_ref, nst_ref, mst_ref, *, tm):
    L = CHUNK

    @pl.when(pl.program_id(1) == 0)
    def _():
        ubuf_ref[0:8, :] = jnp.zeros((8, M_WIDTH), f32)
        cst_ref[...] = jnp.zeros(cst_ref.shape, f32)
        nst_ref[...] = jnp.zeros(nst_ref.shape, f32)
        mst_ref[...] = jnp.zeros(mst_ref.shape, f32)

    x = x_ref[...]
    h = _rms(x, gmix_ref[...]).astype(bf16)
    proj_ref[...] = _dot(h, win_ref[...])

    u = proj_ref[:, C_U:C_U + M_WIDTH]
    ubuf_ref[8:8 + tm, :] = u
    acc = convb_ref[...] + convw_ref[CONV_K - 1:CONV_K, :] * u
    for j in range(CONV_K - 1):
        off = 8 - (CONV_K - 1) + j
        acc = acc + convw_ref[j:j + 1, :] * ubuf_ref[off:off + tm, :]
    ubuf_ref[0:8, :] = u[tm - 8:tm, :]
    proj_ref[:, C_U:C_U + M_WIDTH] = acc * _sigmoid(acc)

    cos = cos_ref[...]
    sin = sin_ref[...]

    def rope(t):
        return t * cos + pltpu.roll(t, 64, 1) * sin

    cqn = _rms(proj_ref[:, C_CQ:C_CQ + Q_LORA], cqg_ref[...]).astype(bf16)
    qa = _dot(cqn, wuq_ref[...])
    ckvn = _rms(proj_ref[:, C_CKV:C_CKV + KV_LORA], ckvg_ref[...]).astype(bf16)
    kv = _dot(ckvn, wukv_ref[...])
    gq_n, gq_r = gq_ref[:, 0:128], gq_ref[:, 128:256]
    gk_n, gk_r = gk_ref[:, 0:128], gk_ref[:, 128:256]
    kr = rope(_rms(proj_ref[:, C_KR:C_KR + 128], gk_r, ROPE_DIM)).astype(bf16)
    scale = A_QK ** -0.5
    for hh in range(A_HEADS):
        c0 = hh * QK_PAD
        qn = _rms(qa[:, c0:c0 + 128], gq_n) * scale
        qr = rope(_rms(qa[:, c0 + 128:c0 + 256], gq_r, ROPE_DIM)) * scale
        q_ref[:, c0:c0 + 128] = qn.astype(bf16)
        q_ref[:, c0 + 128:c0 + 256] = qr.astype(bf16)
        k_ref[:, c0:c0 + 128] = _rms(kv[:, c0:c0 + 128], gk_n).astype(bf16)
        k_ref[:, c0 + 128:c0 + 256] = kr
        v_ref[:, hh * A_DV:(hh + 1) * A_DV] = kv[:, c0 + 128:c0 + 256].astype(bf16)

    row = lax.broadcasted_iota(jnp.int32, (L, L), 0)
    col = lax.broadcasted_iota(jnp.int32, (L, L), 1)
    causal = col <= row
    eye = col == row
    tri = jnp.where(causal, 1.0, 0.0).astype(bf16)
    gbias = gbias_ref[...]

    def to_row(c):
        return jnp.sum(jnp.where(eye, c, 0.0), axis=0, keepdims=True)

    def chunk(c, carry):
        r0 = pl.multiple_of(c * L, L)
        g = proj_ref[pl.ds(r0, L), C_G:C_G + 128] + gbias
        logf = _log_sigmoid(g)
        l1 = logf.astype(bf16)
        r1 = logf - l1.astype(f32)
        l2 = r1.astype(bf16)
        l3 = (r1 - l2.astype(f32)).astype(bf16)
        bcum = _dot(tri, l1) + _dot(tri, l2) + _dot(tri, l3)
        for hh in range(M_HEADS):
            ic = g[:, hh:hh + 1]
            b = bcum[:, M_HEADS + hh:M_HEADS + hh + 1]
            btot = b[L - 1:L, :]
            ucb = proj_ref[pl.ds(r0, L), C_U + hh * M_DV:C_U + (hh + 1) * M_DV].astype(bf16)
            vb = proj_ref[pl.ds(r0, L), C_V + hh * M_DV:C_V + (hh + 1) * M_DV].astype(bf16)
            opre = proj_ref[pl.ds(r0, L), C_O + hh * M_DV:C_O + (hh + 1) * M_DV]
            q = _dot(ucb, wq_ref[hh]) * (M_DK ** -0.5)
            k = _dot(ucb, wk_ref[hh])
            kt = _dot_nt(wkt_ref[hh], ucb)
            qb = q.astype(bf16)

            c_prev = cst_ref[hh]
            n_prev = nst_ref[hh][0:1, :]
            m_prev = mst_ref[hh][0:1, 0:1]

            a = btot - b + ic
            mloc = jnp.max(a, axis=0, keepdims=True)
            wa = jnp.exp(a - mloc)
            r_row = to_row(ic - b)
            wa_row = jnp.exp(r_row + (btot - mloc))
            c_loc = _dot((kt * wa_row).astype(bf16), vb)
            n_loc = jnp.sum(k * wa, axis=0, keepdims=True)

            dlog = jnp.where(causal, b + r_row, -jnp.inf)
            inter = b + m_prev
            mj = jnp.maximum(inter, jnp.max(dlog, axis=-1, keepdims=True))
            dw = jnp.exp(dlog - mj)
            qk = _dot_nt(qb, k.astype(bf16)) * dw
            s_int = jnp.exp(inter - mj)
            num = s_int * _dot(qb, c_prev.astype(bf16)) + _dot(qk.astype(bf16), vb)
            den = (s_int * jnp.sum(q * n_prev, axis=-1, keepdims=True)
                   + jnp.sum(qk, axis=-1, keepdims=True))
            hm = num / jnp.maximum(jnp.abs(den), jnp.exp(-mj))

            m_new = jnp.maximum(btot + m_prev, mloc)
            s_old = jnp.exp(btot + m_prev - m_new)
            s_loc = jnp.exp(mloc - m_new)
            cst_ref[hh] = s_old * c_prev + s_loc * c_loc
            nst_ref[hh] = jnp.broadcast_to(s_old * n_prev + s_loc * n_loc, (8, M_DK))
            mst_ref[hh] = jnp.broadcast_to(m_new, (8, 128))

            y = _sigmoid(opre) * _rms(hm, mog_ref[:, hh * M_DV:(hh + 1) * M_DV])
            ym_ref[pl.ds(r0, L), hh * M_DV:(hh + 1) * M_DV] = y.astype(bf16)
        return carry

    lax.fori_loop(0, tm // L, chunk, 0)


def _front(x, cos, sin, w, B, S, tm):
    T = B * S
    nt = S // tm
    row = lambda b, i: (b * nt + i, 0)
    c2 = lambda b, i: (0, 0)
    c3 = lambda b, i: (0, 0, 0)

    def full(a):
        return pl.BlockSpec(a.shape, c2 if a.ndim == 2 else c3)

    weights = [w["gmix"], w["win"], w["convw"], w["convb"], w["wq"], w["wk"], w["wkt"], w["gbias"],
               w["mog"], w["cqg"], w["ckvg"], w["wuq"], w["wukv"], w["gq"], w["gk"]]
    return pl.pallas_call(
        functools.partial(_front_kernel, tm=tm),
        grid=(B, nt),
        in_specs=[pl.BlockSpec((tm, D_MODEL), row),
                  pl.BlockSpec((tm, 128), row),
                  pl.BlockSpec((tm, 128), row)] + [full(a) for a in weights],
        out_specs=[pl.BlockSpec((tm, M_WIDTH), row),
                   pl.BlockSpec((tm, A_HEADS * QK_PAD), row),
                   pl.BlockSpec((tm, A_HEADS * QK_PAD), row),
                   pl.BlockSpec((tm, A_HEADS * A_DV), row)],
        out_shape=[jax.ShapeDtypeStruct((T, M_WIDTH), bf16),
                   jax.ShapeDtypeStruct((T, A_HEADS * QK_PAD), bf16),
                   jax.ShapeDtypeStruct((T, A_HEADS * QK_PAD), bf16),
                   jax.ShapeDtypeStruct((T, A_HEADS * A_DV), bf16)],
        scratch_shapes=[pltpu.VMEM((tm, C_END), f32),
                        pltpu.VMEM((tm + 8, M_WIDTH), f32),
                        pltpu.VMEM((M_HEADS, M_DK, M_DV), f32),
                        pltpu.VMEM((M_HEADS, 8, M_DK), f32),
                        pltpu.VMEM((M_HEADS, 8, 128), f32)],
        compiler_params=pltpu.CompilerParams(
            dimension_semantics=("arbitrary", "arbitrary"), vmem_limit_bytes=VMEM_LIMIT),
        name="front",
    )(x, cos, sin, *weights)


def _attn_kernel(q_ref, k_ref, v_ref, g_ref, o_ref, *, tq):
    qi = pl.program_id(2)
    q = q_ref[...]

    def block(j, carry, masked):
        m, l, acc = carry
        r0 = pl.multiple_of(j * tq, tq)
        s = _dot_nt(q, k_ref[pl.ds(r0, tq), :])
        if masked:
            row = lax.broadcasted_iota(jnp.int32, (tq, tq), 0)
            col = lax.broadcasted_iota(jnp.int32, (tq, tq), 1)
            s = jnp.where(col <= row, s, -jnp.inf)
        m_new = jnp.maximum(m, jnp.max(s, axis=-1, keepdims=True))
        alpha = jnp.exp(m - m_new)
        p = jnp.exp(s - m_new)
        l = alpha * l + jnp.sum(p, axis=-1, keepdims=True)
        acc = alpha * acc + _dot(p.astype(bf16), v_ref[pl.ds(r0, tq), :])
        return m_new, l, acc

    init = (jnp.full((tq, 1), -jnp.inf, f32), jnp.zeros((tq, 1), f32), jnp.zeros((tq, A_DV), f32))
    carry = lax.fori_loop(0, qi, lambda j, c: block(j, c, False), init)
    m, l, acc = block(qi, carry, True)
    o = acc / l
    o_ref[...] = _rms(o, g_ref[0]).astype(bf16)


def _attention(q, k, v, g, B, S, tq):
    T = B * S
    nq = S // tq
    return pl.pallas_call(
        functools.partial(_attn_kernel, tq=tq),
        grid=(B, A_HEADS, nq),
        in_specs=[pl.BlockSpec((tq, QK_PAD), lambda b, h, i: (b * nq + i, h)),
                  pl.BlockSpec((S, QK_PAD), lambda b, h, i: (b, h)),
                  pl.BlockSpec((S, A_DV), lambda b, h, i: (b, h)),
                  pl.BlockSpec((1, 1, A_DV), lambda b, h, i: (h, 0, 0))],
        out_specs=pl.BlockSpec((tq, A_DV), lambda b, h, i: (b * nq + i, h)),
        out_shape=jax.ShapeDtypeStruct((T, A_HEADS * A_DV), bf16),
        compiler_params=pltpu.CompilerParams(
            dimension_semantics=("arbitrary", "arbitrary", "arbitrary"), vmem_limit_bytes=VMEM_LIMIT),
        name="mla_attention",
    )(q, k, v, g)


def _memkv_kernel(mem_ref, g_ref, w_ref, kg_ref, k_ref, v_ref):
    mn = _rms(mem_ref[...], g_ref[...]).astype(bf16)
    kv = _dot(mn, w_ref[...])
    for hh in range(X_HEADS):
        sl = slice(hh * X_HD, (hh + 1) * X_HD)
        k_ref[:, sl] = _rms(kv[:, sl], kg_ref[...]).astype(bf16)
    v_ref[...] = kv[:, X_WIDTH:].astype(bf16)


def _memkv(mem2d, g, wkv, kg, B, Nm):
    return pl.pallas_call(
        _memkv_kernel,
        grid=(B,),
        in_specs=[pl.BlockSpec((Nm, D_MODEL), lambda b: (b, 0)),
                  pl.BlockSpec((1, D_MODEL), lambda b: (0, 0)),
                  pl.BlockSpec((D_MODEL, 2 * X_WIDTH), lambda b: (0, 0)),
                  pl.BlockSpec((1, X_HD), lambda b: (0, 0))],
        out_specs=[pl.BlockSpec((Nm, X_WIDTH), lambda b: (b, 0)),
                   pl.BlockSpec((Nm, X_WIDTH), lambda b: (b, 0))],
        out_shape=[jax.ShapeDtypeStruct((B * Nm, X_WIDTH), bf16)] * 2,
        compiler_params=pltpu.CompilerParams(
            dimension_semantics=("arbitrary",), vmem_limit_bytes=VMEM_LIMIT),
        name="mem_kv",
    )(mem2d, g, wkv, kg)


def _back_kernel(x_ref, ym_ref, ya_ref, wout_ref, gx_ref, wqx_ref, xqg_ref, kx_ref, vx_ref,
                 wox_ref, gf_ref, w1_ref, w2_ref, out_ref):
    x1 = (x_ref[...] + _dot(ym_ref[...], wout_ref[0:M_WIDTH, :])
          + _dot(ya_ref[...], wout_ref[M_WIDTH:2 * M_WIDTH, :]))

    hn = _rms(x1, gx_ref[...]).astype(bf16)
    qx = _dot(hn, wqx_ref[...])
    heads = []
    for hh in range(X_HEADS):
        sl = slice(hh * X_HD, (hh + 1) * X_HD)
        qh = (_rms(qx[:, sl], xqg_ref[...]) * (X_HD ** -0.5)).astype(bf16)
        s = _dot_nt(qh, kx_ref[:, sl])
        p = jnp.exp(s - jnp.max(s, axis=-1, keepdims=True))
        o = _dot(p.astype(bf16), vx_ref[:, sl]) / jnp.sum(p, axis=-1, keepdims=True)
        heads.append(o.astype(bf16))
    x2 = x1 + _dot(jnp.concatenate(heads, axis=-1), wox_ref[...])

    hf = _rms(x2, gf_ref[...]).astype(bf16)
    acc = x2
    for c in range(D_FF // FF_CHUNK):
        sl = slice(c * FF_CHUNK, (c + 1) * FF_CHUNK)
        a = jnp.maximum(_dot(hf, w1_ref[:, sl]), 0.0)
        acc = acc + _dot((a * a).astype(bf16), w2_ref[sl, :])
    out_ref[...] = acc


def _back(x, ym, ya, kx, vx, w, B, S, Nm, tm):
    T = B * S
    row = lambda i: (i, 0)
    c2 = lambda i: (0, 0)
    per_b = lambda i: ((i * tm) // S, 0)

    def const(a):
        return pl.BlockSpec(a.shape, c2, pipeline_mode=pl.Buffered(1))

    return pl.pallas_call(
        _back_kernel,
        grid=(T // tm,),
        in_specs=[pl.BlockSpec((tm, D_MODEL), row),
                  pl.BlockSpec((tm, M_WIDTH), row),
                  pl.BlockSpec((tm, M_WIDTH), row),
                  const(w["wout"]), const(w["gx"]), const(w["wqx"]), const(w["xqg"]),
                  pl.BlockSpec((Nm, X_WIDTH), per_b),
                  pl.BlockSpec((Nm, X_WIDTH), per_b),
                  const(w["wox"]), const(w["gf"]), const(w["w1"]), const(w["w2"])],
        out_specs=pl.BlockSpec((tm, D_MODEL), row),
        out_shape=jax.ShapeDtypeStruct((T, D_MODEL), f32),
        compiler_params=pltpu.CompilerParams(
            dimension_semantics=("arbitrary",), vmem_limit_bytes=VMEM_LIMIT),
        name="back",
    )(x, ym, ya, w["wout"], w["gx"], w["wqx"], w["xqg"], kx, vx, w["wox"], w["gf"], w["w1"], w["w2"])


def _rope_pad(a):
    z = jnp.zeros(a.shape[:-1] + (ROPE_DIM // 2,), a.dtype)
    return jnp.concatenate([a[..., :32], z, a[..., 32:], z], axis=-1)


def _layer_weights(l, norm_mix_g, w_in, conv_w, conv_b, wq_m, wk_m, b_igate, b_fgate, m_out_g,
                   cq_norm_g, ckv_norm_g, w_uq, w_ukv, qk_norm_q, qk_norm_k, a_out_g, w_out,
                   norm_x_g, norm_mem_g, wq_x, wkv_x, xq_norm_g, xk_norm_g, wo_x,
                   norm_ffn_g, w_ff1, w_ff2):
    wi = w_in[l]
    o_g = 3 * M_WIDTH
    o_cq = o_g + 2 * M_HEADS
    o_ckv = o_cq + Q_LORA
    o_kr = o_ckv + KV_LORA
    gates = jnp.pad(wi[:, o_g:o_cq], ((0, 0), (0, 128 - 2 * M_HEADS)))
    win = jnp.concatenate([wi[:, :o_g], gates, wi[:, o_cq:o_kr], _rope_pad(wi[:, o_kr:])], axis=1)
    wuq = w_uq[l].reshape(Q_LORA, A_HEADS, A_QK)
    wuq = jnp.concatenate([wuq[..., :NOPE_DIM], _rope_pad(wuq[..., NOPE_DIM:])], axis=-1)
    gbias = jnp.pad(jnp.concatenate([b_igate[l], b_fgate[l]]), (0, 128 - 2 * M_HEADS))[None, :]
    return {
        "gmix": norm_mix_g[l][None, :],
        "win": win.astype(bf16),
        "convw": conv_w[l],
        "convb": conv_b[l][None, :],
        "wq": wq_m[l].astype(bf16),
        "wk": wk_m[l].astype(bf16),
        "wkt": jnp.swapaxes(wk_m[l], 1, 2).astype(bf16),
        "gbias": gbias,
        "mog": m_out_g[l].reshape(1, M_WIDTH),
        "cqg": cq_norm_g[l][None, :],
        "ckvg": ckv_norm_g[l][None, :],
        "wuq": wuq.reshape(Q_LORA, A_HEADS * QK_PAD).astype(bf16),
        "wukv": w_ukv[l].astype(bf16),
        "gq": jnp.concatenate([qk_norm_q[l][:NOPE_DIM], _rope_pad(qk_norm_q[l][NOPE_DIM:])])[None, :],
        "gk": jnp.concatenate([qk_norm_k[l][:NOPE_DIM], _rope_pad(qk_norm_k[l][NOPE_DIM:])])[None, :],
        "aog": a_out_g[l].reshape(A_HEADS, 1, A_DV),
        "wout": w_out[l].astype(bf16),
        "gx": norm_x_g[l][None, :],
        "gmem": norm_mem_g[l][None, :],
        "wqx": wq_x[l].astype(bf16),
        "wkvx": wkv_x[l].astype(bf16),
        "xqg": xq_norm_g[l][None, :],
        "xkg": xk_norm_g[l][None, :],
        "wox": wo_x[l].astype(bf16),
        "gf": norm_ffn_g[l][None, :],
        "w1": w_ff1[l].astype(bf16),
        "w2": w_ff2[l].astype(bf16),
    }


def kernel(x, mem, positions, norm_mix_g, w_in, conv_w, conv_b, wq_m, wk_m, b_igate, b_fgate, m_out_g, cq_norm_g, ckv_norm_g, w_uq, w_ukv, qk_norm_q, qk_norm_k, a_out_g, w_out, norm_x_g, norm_mem_g, wq_x, wkv_x, xq_norm_g, xk_norm_g, wo_x, norm_ffn_g, w_ff1, w_ff2):
    B, S, D = x.shape
    Nm = mem.shape[1]
    depth = w_in.shape[0]
    assert D == D_MODEL and S % CHUNK == 0
    tm_front = min(S, 512)
    tq = min(S, 256)
    tm_back = min(S, 512)
    params = (norm_mix_g, w_in, conv_w, conv_b, wq_m, wk_m, b_igate, b_fgate, m_out_g,
              cq_norm_g, ckv_norm_g, w_uq, w_ukv, qk_norm_q, qk_norm_k, a_out_g, w_out,
              norm_x_g, norm_mem_g, wq_x, wkv_x, xq_norm_g, xk_norm_g, wo_x,
              norm_ffn_g, w_ff1, w_ff2)

    cos, sin = _rope_tables(positions)
    xt = x.reshape(B * S, D)
    mem2d = mem.reshape(B * Nm, D)
    for l in range(depth):
        w = _layer_weights(l, *params)
        ym, q, k, v = _front(xt, cos, sin, w, B, S, tm_front)
        ya = _attention(q, k, v, w["aog"], B, S, tq)
        kx, vx = _memkv(mem2d, w["gmem"], w["wkvx"], w["xkg"], B, Nm)
        xt = _back(xt, ym, ya, kx, vx, w, B, S, Nm, tm_back)
    return xt.reshape(B, S, D)
```

```python
import functools

import jax
import jax.numpy as jnp
from jax import lax
from jax.experimental import pallas as pl
from jax.experimental.pallas import tpu as pltpu

f32 = jnp.float32
bf16 = jnp.bfloat16

EPS = 1e-6
D_MODEL = 1024
M_HEADS = 4
M_WIDTH = 512
M_DV = 128
M_DK = 64
CONV_K = 4
CHUNK = 64
A_HEADS = 4
A_DV = 128
Q_LORA = 256
KV_LORA = 128
NOPE_DIM = 128
ROPE_DIM = 64
A_QK = NOPE_DIM + ROPE_DIM
ROPE_THETA = 10000.0
X_HEADS = 4
X_HD = 128
X_WIDTH = 512
D_FF = 4096
FF_CHUNK = 1024

QK_PAD = 256
C_U, C_V, C_O, C_G, C_CQ, C_CKV, C_KR, C_END = 0, 512, 1024, 1536, 1664, 1920, 2048, 2176

VMEM_LIMIT = 56 * 1024 * 1024
LOG2E = 1.4426950408889634
ONES_ROWS = 16


def _dot(a, b):
    return jnp.dot(a, b, preferred_element_type=f32)


def _dot_nt(a, b):
    return lax.dot_general(a, b, (((1,), (1,)), ((), ())), preferred_element_type=f32)


def _rms(x, g, n=None):
    n = x.shape[-1] if n is None else n
    ms = jnp.sum(x * x, axis=-1, keepdims=True) * (1.0 / n)
    return x * lax.rsqrt(ms + EPS) * g


def _sigmoid(x):
    return 1.0 / (1.0 + jnp.exp(-x))


def _log_sigmoid(x):
    return jnp.minimum(x, 0.0) - jnp.log1p(jnp.exp(-jnp.abs(x)))


def _rope_kernel(pos_ref, inv_ref, sgn_ref, cos_ref, sin_ref):
    ang = pos_ref[...].astype(f32) * inv_ref[...]
    cos_ref[...] = jnp.cos(ang) * jnp.abs(sgn_ref[...])
    sin_ref[...] = jnp.sin(ang) * sgn_ref[...]


def _rope_tables(positions):
    T = positions.size
    tm = min(T, 2048)
    inv = 1.0 / (ROPE_THETA ** (jnp.arange(0, ROPE_DIM, 2, dtype=f32) / ROPE_DIM))
    z = jnp.zeros((ROPE_DIM // 2,), f32)
    o = jnp.ones((ROPE_DIM // 2,), f32)
    inv_p = jnp.concatenate([inv, z, inv, z])[None, :]
    sgn = jnp.concatenate([-o, z, o, z])[None, :]
    return pl.pallas_call(
        _rope_kernel,
        grid=(T // tm,),
        in_specs=[pl.BlockSpec((tm, 1), lambda i: (i, 0)),
                  pl.BlockSpec((1, 128), lambda i: (0, 0)),
                  pl.BlockSpec((1, 128), lambda i: (0, 0))],
        out_specs=[pl.BlockSpec((tm, 128), lambda i: (i, 0)),
                   pl.BlockSpec((tm, 128), lambda i: (i, 0))],
        out_shape=[jax.ShapeDtypeStruct((T, 128), f32)] * 2,
        name="rope_tables",
    )(positions.reshape(T, 1), inv_p, sgn)


def _front_kernel(x_ref, cos_ref, sin_ref, gmix_ref, win_ref, convw_ref, convb_ref,
                  wq_ref, wk_ref, wkt_ref, gbias_ref, mog_ref, cqg_ref, ckvg_ref,
                  wuq_ref, wukvk_ref, wukvvt_ref, gq_ref, gk_ref,
                  ym_ref, q_ref, k_ref, vt_ref,
                  proj_ref, ubuf_ref, cst_ref, nst_ref, mst_ref, *, tm, tk):
    L = CHUNK

    @pl.when(pl.program_id(1) == 0)
    def _():
        ubuf_ref[0:8, :] = jnp.zeros((8, M_WIDTH), f32)
        cst_ref[...] = jnp.zeros(cst_ref.shape, f32)
        nst_ref[...] = jnp.zeros(nst_ref.shape, f32)
        mst_ref[...] = jnp.zeros(mst_ref.shape, f32)

    x = x_ref[...]
    h = _rms(x, gmix_ref[...]).astype(bf16)
    proj_ref[...] = _dot(h, win_ref[...])

    u = proj_ref[:, C_U:C_U + M_WIDTH]
    ubuf_ref[8:8 + tm, :] = u
    acc = convb_ref[...] + convw_ref[CONV_K - 1:CONV_K, :] * u
    for j in range(CONV_K - 1):
        off = 8 - (CONV_K - 1) + j
        acc = acc + convw_ref[j:j + 1, :] * ubuf_ref[off:off + tm, :]
    ubuf_ref[0:8, :] = u[tm - 8:tm, :]
    proj_ref[:, C_U:C_U + M_WIDTH] = acc * _sigmoid(acc)

    cos = cos_ref[...]
    sin = sin_ref[...]

    def rope(t):
        return t * cos + pltpu.roll(t, 64, 1) * sin

    cqn = _rms(proj_ref[:, C_CQ:C_CQ + Q_LORA], cqg_ref[...]).astype(bf16)
    qa = _dot(cqn, wuq_ref[...])
    ckvn = _rms(proj_ref[:, C_CKV:C_CKV + KV_LORA], ckvg_ref[...]).astype(bf16)
    kn = _dot(ckvn, wukvk_ref[...])
    vt = _dot_nt(wukvvt_ref[...], ckvn)
    for c in range(tm // tk):
        vt_ref[c * A_HEADS * A_DV:(c + 1) * A_HEADS * A_DV, :] = vt[:, c * tk:(c + 1) * tk].astype(bf16)
    gq_n, gq_r = gq_ref[:, 0:128], gq_ref[:, 128:256]
    gk_n, gk_r = gk_ref[:, 0:128], gk_ref[:, 128:256]
    kr = rope(_rms(proj_ref[:, C_KR:C_KR + 128], gk_r, ROPE_DIM)).astype(bf16)
    scale = A_QK ** -0.5 * LOG2E
    for hh in range(A_HEADS):
        c0 = hh * QK_PAD
        qn = _rms(qa[:, c0:c0 + 128], gq_n) * scale
        qr = rope(_rms(qa[:, c0 + 128:c0 + 256], gq_r, ROPE_DIM)) * scale
        q_ref[:, c0:c0 + 128] = qn.astype(bf16)
        q_ref[:, c0 + 128:c0 + 256] = qr.astype(bf16)
        k_ref[:, c0:c0 + 128] = _rms(kn[:, hh * 128:(hh + 1) * 128], gk_n).astype(bf16)
        k_ref[:, c0 + 128:c0 + 256] = kr

    row = lax.broadcasted_iota(jnp.int32, (L, L), 0)
    col = lax.broadcasted_iota(jnp.int32, (L, L), 1)
    causal = col <= row
    eye = col == row
    tri = jnp.where(causal, 1.0, 0.0).astype(bf16)
    gbias = gbias_ref[...]

    def to_row(c):
        return jnp.sum(jnp.where(eye, c, 0.0), axis=0, keepdims=True)

    def chunk(c, carry):
        r0 = pl.multiple_of(c * L, L)
        g = proj_ref[pl.ds(r0, L), C_G:C_G + 128] + gbias
        logf = _log_sigmoid(g)
        l1 = logf.astype(bf16)
        r1 = logf - l1.astype(f32)
        l2 = r1.astype(bf16)
        l3 = (r1 - l2.astype(f32)).astype(bf16)
        bcum = _dot(tri, l1) + _dot(tri, l2) + _dot(tri, l3)
        for hh in range(M_HEADS):
            ic = g[:, hh:hh + 1]
            b = bcum[:, M_HEADS + hh:M_HEADS + hh + 1]
            btot = b[L - 1:L, :]
            ucb = proj_ref[pl.ds(r0, L), C_U + hh * M_DV:C_U + (hh + 1) * M_DV].astype(bf16)
            vb = proj_ref[pl.ds(r0, L), C_V + hh * M_DV:C_V + (hh + 1) * M_DV].astype(bf16)
            opre = proj_ref[pl.ds(r0, L), C_O + hh * M_DV:C_O + (hh + 1) * M_DV]
            q = _dot(ucb, wq_ref[hh]) * (M_DK ** -0.5)
            k = _dot(ucb, wk_ref[hh])
            kt = _dot_nt(wkt_ref[hh], ucb)
            qb = q.astype(bf16)

            c_prev = cst_ref[hh]
            n_prev = nst_ref[hh][0:1, :]
            m_prev = mst_ref[hh][0:1, 0:1]

            a = btot - b + ic
            mloc = jnp.max(a, axis=0, keepdims=True)
            wa = jnp.exp(a - mloc)
            r_row = to_row(ic - b)
            wa_row = jnp.exp(r_row + (btot - mloc))
            c_loc = _dot((kt * wa_row).astype(bf16), vb)
            n_loc = jnp.sum(k * wa, axis=0, keepdims=True)

            dlog = jnp.where(causal, b + r_row, -jnp.inf)
            inter = b + m_prev
            mj = jnp.maximum(inter, jnp.max(dlog, axis=-1, keepdims=True))
            dw = jnp.exp(dlog - mj)
            qk = _dot_nt(qb, k.astype(bf16)) * dw
            s_int = jnp.exp(inter - mj)
            num = s_int * _dot(qb, c_prev.astype(bf16)) + _dot(qk.astype(bf16), vb)
            den = (s_int * jnp.sum(q * n_prev, axis=-1, keepdims=True)
                   + jnp.sum(qk, axis=-1, keepdims=True))
            hm = num / jnp.maximum(jnp.abs(den), jnp.exp(-mj))

            m_new = jnp.maximum(btot + m_prev, mloc)
            s_old = jnp.exp(btot + m_prev - m_new)
            s_loc = jnp.exp(mloc - m_new)
            cst_ref[hh] = s_old * c_prev + s_loc * c_loc
            nst_ref[hh] = jnp.broadcast_to(s_old * n_prev + s_loc * n_loc, (8, M_DK))
            mst_ref[hh] = jnp.broadcast_to(m_new, (8, 128))

            y = _sigmoid(opre) * _rms(hm, mog_ref[:, hh * M_DV:(hh + 1) * M_DV])
            ym_ref[pl.ds(r0, L), hh * M_DV:(hh + 1) * M_DV] = y.astype(bf16)
        return carry

    lax.fori_loop(0, tm // L, chunk, 0)


def _front(x, cos, sin, w, B, S, tm, tk):
    T = B * S
    nt = S // tm
    vt_rows = (tm // tk) * A_HEADS * A_DV
    row = lambda b, i: (b * nt + i, 0)
    c2 = lambda b, i: (0, 0)
    c3 = lambda b, i: (0, 0, 0)

    def full(a):
        return pl.BlockSpec(a.shape, c2 if a.ndim == 2 else c3)

    weights = [w["gmix"], w["win"], w["convw"], w["convb"], w["wq"], w["wk"], w["wkt"], w["gbias"],
               w["mog"], w["cqg"], w["ckvg"], w["wuq"], w["wukvk"], w["wukvvt"], w["gq"], w["gk"]]
    return pl.pallas_call(
        functools.partial(_front_kernel, tm=tm, tk=tk),
        grid=(B, nt),
        in_specs=[pl.BlockSpec((tm, D_MODEL), row),
                  pl.BlockSpec((tm, 128), row),
                  pl.BlockSpec((tm, 128), row)] + [full(a) for a in weights],
        out_specs=[pl.BlockSpec((tm, M_WIDTH), row),
                   pl.BlockSpec((tm, A_HEADS * QK_PAD), row),
                   pl.BlockSpec((tm, A_HEADS * QK_PAD), row),
                   pl.BlockSpec((vt_rows, tk), row)],
        out_shape=[jax.ShapeDtypeStruct((T, M_WIDTH), bf16),
                   jax.ShapeDtypeStruct((T, A_HEADS * QK_PAD), bf16),
                   jax.ShapeDtypeStruct((T, A_HEADS * QK_PAD), bf16),
                   jax.ShapeDtypeStruct((B * nt * vt_rows, tk), bf16)],
        scratch_shapes=[pltpu.VMEM((tm, C_END), f32),
                        pltpu.VMEM((tm + 8, M_WIDTH), f32),
                        pltpu.VMEM((M_HEADS, M_DK, M_DV), f32),
                        pltpu.VMEM((M_HEADS, 8, M_DK), f32),
                        pltpu.VMEM((M_HEADS, 8, 128), f32)],
        compiler_params=pltpu.CompilerParams(
            dimension_semantics=("arbitrary", "arbitrary"), vmem_limit_bytes=VMEM_LIMIT),
        name="front",
    )(x, cos, sin, *weights)


def _attn_kernel(q_ref, k_ref, vt_ref, g_ref, o_ref, acc_ref, *, tq):
    qi = pl.program_id(1)
    ones = jnp.ones((ONES_ROWS, tq), bf16)
    acc_ref[...] = jnp.zeros(acc_ref.shape, f32)

    def block(j, ms, masked):
        r0 = pl.multiple_of(j * tq, tq)
        v0 = j * (A_HEADS * A_DV)
        out = []
        for hh in range(A_HEADS):
            c0 = hh * QK_PAD
            s = _dot_nt(k_ref[pl.ds(r0, tq), c0:c0 + QK_PAD], q_ref[:, c0:c0 + QK_PAD])
            if masked:
                row = lax.broadcasted_iota(jnp.int32, (tq, tq), 0)
                col = lax.broadcasted_iota(jnp.int32, (tq, tq), 1)
                s = jnp.where(row <= col, s, -jnp.inf)
            m_new = jnp.maximum(ms[hh], jnp.max(s, axis=0, keepdims=True))
            alpha = jnp.exp2(ms[hh] - m_new)
            p = jnp.exp2(s - m_new).astype(bf16)
            vte = jnp.concatenate(
                [vt_ref[pl.ds(pl.multiple_of(v0 + hh * A_DV, A_DV), A_DV), :], ones], axis=0)
            acc_ref[hh] = alpha * acc_ref[hh] + _dot(vte, p)
            out.append(m_new)
        return tuple(out)

    ms = tuple(jnp.full((1, tq), -jnp.inf, f32) for _ in range(A_HEADS))
    ms = lax.fori_loop(0, qi, lambda j, c: block(j, c, False), ms)
    block(qi, ms, True)
    for hh in range(A_HEADS):
        a = acc_ref[hh]
        o = a[0:A_DV, :] / a[A_DV:A_DV + 1, :]
        ms2 = jnp.sum(o * o, axis=0, keepdims=True) * (1.0 / A_DV)
        y = o * lax.rsqrt(ms2 + EPS) * g_ref[hh * A_DV:(hh + 1) * A_DV, :]
        o_ref[:, hh * A_DV:(hh + 1) * A_DV] = y.T.astype(bf16)


def _attention(q, k, vt, g, B, S, tq):
    T = B * S
    nq = S // tq
    return pl.pallas_call(
        functools.partial(_attn_kernel, tq=tq),
        grid=(B, nq),
        in_specs=[pl.BlockSpec((tq, A_HEADS * QK_PAD), lambda b, i: (b * nq + i, 0)),
                  pl.BlockSpec((S, A_HEADS * QK_PAD), lambda b, i: (b, 0)),
                  pl.BlockSpec((nq * A_HEADS * A_DV, tq), lambda b, i: (b, 0)),
                  pl.BlockSpec((A_HEADS * A_DV, 1), lambda b, i: (0, 0))],
        out_specs=pl.BlockSpec((tq, A_HEADS * A_DV), lambda b, i: (b * nq + i, 0)),
        out_shape=jax.ShapeDtypeStruct((T, A_HEADS * A_DV), bf16),
        scratch_shapes=[pltpu.VMEM((A_HEADS, A_DV + ONES_ROWS, tq), f32)],
        compiler_params=pltpu.CompilerParams(
            dimension_semantics=("arbitrary", "arbitrary"), vmem_limit_bytes=VMEM_LIMIT),
        name="mla_attention",
    )(q, k, vt, g)


def _memkv_kernel(mem_ref, g_ref, w_ref, kg_ref, k_ref, v_ref):
    mn = _rms(mem_ref[...], g_ref[...]).astype(bf16)
    kv = _dot(mn, w_ref[...])
    for hh in range(X_HEADS):
        sl = slice(hh * X_HD, (hh + 1) * X_HD)
        k_ref[:, sl] = _rms(kv[:, sl], kg_ref[...]).astype(bf16)
    v_ref[...] = kv[:, X_WIDTH:].astype(bf16)


def _memkv(mem2d, g, wkv, kg, B, Nm):
    return pl.pallas_call(
        _memkv_kernel,
        grid=(B,),
        in_specs=[pl.BlockSpec((Nm, D_MODEL), lambda b: (b, 0)),
                  pl.BlockSpec((1, D_MODEL), lambda b: (0, 0)),
                  pl.BlockSpec((D_MODEL, 2 * X_WIDTH), lambda b: (0, 0)),
                  pl.BlockSpec((1, X_HD), lambda b: (0, 0))],
        out_specs=[pl.BlockSpec((Nm, X_WIDTH), lambda b: (b, 0)),
                   pl.BlockSpec((Nm, X_WIDTH), lambda b: (b, 0))],
        out_shape=[jax.ShapeDtypeStruct((B * Nm, X_WIDTH), bf16)] * 2,
        compiler_params=pltpu.CompilerParams(
            dimension_semantics=("arbitrary",), vmem_limit_bytes=VMEM_LIMIT),
        name="mem_kv",
    )(mem2d, g, wkv, kg)


def _back_kernel(x_ref, ym_ref, ya_ref, wout_ref, gx_ref, wqx_ref, xqg_ref, kx_ref, vx_ref,
                 wox_ref, gf_ref, w1_ref, w2_ref, out_ref):
    x1 = (x_ref[...] + _dot(ym_ref[...], wout_ref[0:M_WIDTH, :])
          + _dot(ya_ref[...], wout_ref[M_WIDTH:2 * M_WIDTH, :]))

    hn = _rms(x1, gx_ref[...]).astype(bf16)
    qx = _dot(hn, wqx_ref[...])
    heads = []
    for hh in range(X_HEADS):
        sl = slice(hh * X_HD, (hh + 1) * X_HD)
        qh = (_rms(qx[:, sl], xqg_ref[...]) * (X_HD ** -0.5)).astype(bf16)
        s = _dot_nt(qh, kx_ref[:, sl])
        p = jnp.exp(s - jnp.max(s, axis=-1, keepdims=True))
        o = _dot(p.astype(bf16), vx_ref[:, sl]) / jnp.sum(p, axis=-1, keepdims=True)
        heads.append(o.astype(bf16))
    x2 = x1 + _dot(jnp.concatenate(heads, axis=-1), wox_ref[...])

    hf = _rms(x2, gf_ref[...]).astype(bf16)
    acc = x2
    for c in range(D_FF // FF_CHUNK):
        sl = slice(c * FF_CHUNK, (c + 1) * FF_CHUNK)
        a = jnp.maximum(_dot(hf, w1_ref[:, sl]), 0.0)
        acc = acc + _dot((a * a).astype(bf16), w2_ref[sl, :])
    out_ref[...] = acc


def _back(x, ym, ya, kx, vx, w, B, S, Nm, tm):
    T = B * S
    row = lambda i: (i, 0)
    c2 = lambda i: (0, 0)
    per_b = lambda i: ((i * tm) // S, 0)

    def const(a):
        return pl.BlockSpec(a.shape, c2, pipeline_mode=pl.Buffered(1))

    return pl.pallas_call(
        _back_kernel,
        grid=(T // tm,),
        in_specs=[pl.BlockSpec((tm, D_MODEL), row),
                  pl.BlockSpec((tm, M_WIDTH), row),
                  pl.BlockSpec((tm, M_WIDTH), row),
                  const(w["wout"]), const(w["gx"]), const(w["wqx"]), const(w["xqg"]),
                  pl.BlockSpec((Nm, X_WIDTH), per_b),
                  pl.BlockSpec((Nm, X_WIDTH), per_b),
                  const(w["wox"]), const(w["gf"]), const(w["w1"]), const(w["w2"])],
        out_specs=pl.BlockSpec((tm, D_MODEL), row),
        out_shape=jax.ShapeDtypeStruct((T, D_MODEL), f32),
        compiler_params=pltpu.CompilerParams(
            dimension_semantics=("arbitrary",), vmem_limit_bytes=VMEM_LIMIT),
        name="back",
    )(x, ym, ya, w["wout"], w["gx"], w["wqx"], w["xqg"], kx, vx, w["wox"], w["gf"], w["w1"], w["w2"])


def _rope_pad(a):
    z = jnp.zeros(a.shape[:-1] + (ROPE_DIM // 2,), a.dtype)
    return jnp.concatenate([a[..., :32], z, a[..., 32:], z], axis=-1)


def _layer_weights(l, norm_mix_g, w_in, conv_w, conv_b, wq_m, wk_m, b_igate, b_fgate, m_out_g,
                   cq_norm_g, ckv_norm_g, w_uq, w_ukv, qk_norm_q, qk_norm_k, a_out_g, w_out,
                   norm_x_g, norm_mem_g, wq_x, wkv_x, xq_norm_g, xk_norm_g, wo_x,
                   norm_ffn_g, w_ff1, w_ff2):
    wi = w_in[l]
    o_g = 3 * M_WIDTH
    o_cq = o_g + 2 * M_HEADS
    o_ckv = o_cq + Q_LORA
    o_kr = o_ckv + KV_LORA
    gates = jnp.pad(wi[:, o_g:o_cq], ((0, 0), (0, 128 - 2 * M_HEADS)))
    win = jnp.concatenate([wi[:, :o_g], gates, wi[:, o_cq:o_kr], _rope_pad(wi[:, o_kr:])], axis=1)
    wuq = w_uq[l].reshape(Q_LORA, A_HEADS, A_QK)
    wuq = jnp.concatenate([wuq[..., :NOPE_DIM], _rope_pad(wuq[..., NOPE_DIM:])], axis=-1)
    wukv = w_ukv[l].reshape(KV_LORA, A_HEADS, NOPE_DIM + A_DV)
    gbias = jnp.pad(jnp.concatenate([b_igate[l], b_fgate[l]]), (0, 128 - 2 * M_HEADS))[None, :]
    return {
        "gmix": norm_mix_g[l][None, :],
        "win": win.astype(bf16),
        "convw": conv_w[l],
        "convb": conv_b[l][None, :],
        "wq": wq_m[l].astype(bf16),
        "wk": wk_m[l].astype(bf16),
        "wkt": jnp.swapaxes(wk_m[l], 1, 2).astype(bf16),
        "gbias": gbias,
        "mog": m_out_g[l].reshape(1, M_WIDTH),
        "cqg": cq_norm_g[l][None, :],
        "ckvg": ckv_norm_g[l][None, :],
        "wuq": wuq.reshape(Q_LORA, A_HEADS * QK_PAD).astype(bf16),
        "wukvk": wukv[..., :NOPE_DIM].reshape(KV_LORA, A_HEADS * NOPE_DIM).astype(bf16),
        "wukvvt": wukv[..., NOPE_DIM:].reshape(KV_LORA, A_HEADS * A_DV).T.astype(bf16),
        "gq": jnp.concatenate([qk_norm_q[l][:NOPE_DIM], _rope_pad(qk_norm_q[l][NOPE_DIM:])])[None, :],
        "gk": jnp.concatenate([qk_norm_k[l][:NOPE_DIM], _rope_pad(qk_norm_k[l][NOPE_DIM:])])[None, :],
        "aog": a_out_g[l].reshape(A_HEADS * A_DV, 1),
        "wout": w_out[l].astype(bf16),
        "gx": norm_x_g[l][None, :],
        "gmem": norm_mem_g[l][None, :],
        "wqx": wq_x[l].astype(bf16),
        "wkvx": wkv_x[l].astype(bf16),
        "xqg": xq_norm_g[l][None, :],
        "xkg": xk_norm_g[l][None, :],
        "wox": wo_x[l].astype(bf16),
        "gf": norm_ffn_g[l][None, :],
        "w1": w_ff1[l].astype(bf16),
        "w2": w_ff2[l].astype(bf16),
    }


def kernel(x, mem, positions, norm_mix_g, w_in, conv_w, conv_b, wq_m, wk_m, b_igate, b_fgate, m_out_g, cq_norm_g, ckv_norm_g, w_uq, w_ukv, qk_norm_q, qk_norm_k, a_out_g, w_out, norm_x_g, norm_mem_g, wq_x, wkv_x, xq_norm_g, xk_norm_g, wo_x, norm_ffn_g, w_ff1, w_ff2):
    B, S, D = x.shape
    Nm = mem.shape[1]
    depth = w_in.shape[0]
    assert D == D_MODEL and S % CHUNK == 0
    tm_front = min(S, 512)
    tq = min(S, 256)
    tm_back = min(S, 512)
    params = (norm_mix_g, w_in, conv_w, conv_b, wq_m, wk_m, b_igate, b_fgate, m_out_g,
              cq_norm_g, ckv_norm_g, w_uq, w_ukv, qk_norm_q, qk_norm_k, a_out_g, w_out,
              norm_x_g, norm_mem_g, wq_x, wkv_x, xq_norm_g, xk_norm_g, wo_x,
              norm_ffn_g, w_ff1, w_ff2)

    cos, sin = _rope_tables(positions)
    xt = x.reshape(B * S, D)
    mem2d = mem.reshape(B * Nm, D)
    for l in range(depth):
        w = _layer_weights(l, *params)
        ym, q, k, vt = _front(xt, cos, sin, w, B, S, tm_front, tq)
        ya = _attention(q, k, vt, w["aog"], B, S, tq)
        kx, vx = _memkv(mem2d, w["gmem"], w["wkvx"], w["xkg"], B, Nm)
        xt = _back(xt, ym, ya, kx, vx, w, B, S, Nm, tm_back)
    return xt.reshape(B, S, D)
```

```python
import functools

import jax
import jax.numpy as jnp
from jax import lax
from jax.experimental import pallas as pl
from jax.experimental.pallas import tpu as pltpu

f32 = jnp.float32
bf16 = jnp.bfloat16

EPS = 1e-6
D_MODEL = 1024
M_HEADS = 4
M_WIDTH = 512
M_DV = 128
M_DK = 64
CONV_K = 4
A_HEADS = 4
A_DV = 128
Q_LORA = 256
KV_LORA = 128
NOPE_DIM = 128
ROPE_DIM = 64
A_QK = NOPE_DIM + ROPE_DIM
ROPE_THETA = 10000.0
X_HEADS = 4
X_HD = 128
X_WIDTH = 512
D_FF = 4096
FF_CHUNK = 1024

M_CHUNK = 128
M_DKP = 128
SUB = 256
QK_PAD = 256
ONES_ROWS = 16
N_U, N_O, N_CQ, N_CKV, N_KR, N_END = 0, 512, 1024, 1280, 1408, 1536
T_V, T_GI, T_GF, T_END = 0, 512, 520, 528

VMEM_LIMIT = 56 * 1024 * 1024
LOG2E = 1.4426950408889634


def _dot(a, b):
    return jnp.dot(a, b, preferred_element_type=f32)


def _dot_nt(a, b):
    return lax.dot_general(a, b, (((1,), (1,)), ((), ())), preferred_element_type=f32)


def _rms(x, g, n=None):
    n = x.shape[-1] if n is None else n
    ms = jnp.sum(x * x, axis=-1, keepdims=True) * (1.0 / n)
    return x * lax.rsqrt(ms + EPS) * g


def _sigmoid(x):
    return 1.0 / (1.0 + jnp.exp(-x))


def _log_sigmoid(x):
    return jnp.minimum(x, 0.0) - jnp.log1p(jnp.exp(-jnp.abs(x)))


def _split3(x):
    a = x.astype(bf16)
    r = x - a.astype(f32)
    b = r.astype(bf16)
    c = (r - b.astype(f32)).astype(bf16)
    return a, b, c


def _rope_kernel(pos_ref, inv_ref, sgn_ref, cos_ref, sin_ref):
    ang = pos_ref[...].astype(f32) * inv_ref[...]
    cos_ref[...] = jnp.cos(ang) * jnp.abs(sgn_ref[...])
    sin_ref[...] = jnp.sin(ang) * sgn_ref[...]


def _rope_tables(positions):
    T = positions.size
    tm = min(T, 2048)
    inv = 1.0 / (ROPE_THETA ** (jnp.arange(0, ROPE_DIM, 2, dtype=f32) / ROPE_DIM))
    z = jnp.zeros((ROPE_DIM // 2,), f32)
    o = jnp.ones((ROPE_DIM // 2,), f32)
    inv_p = jnp.concatenate([inv, z, inv, z])[None, :]
    sgn = jnp.concatenate([-o, z, o, z])[None, :]
    return pl.pallas_call(
        _rope_kernel,
        grid=(T // tm,),
        in_specs=[pl.BlockSpec((tm, 1), lambda i: (i, 0)),
                  pl.BlockSpec((1, 128), lambda i: (0, 0)),
                  pl.BlockSpec((1, 128), lambda i: (0, 0))],
        out_specs=[pl.BlockSpec((tm, 128), lambda i: (i, 0)),
                   pl.BlockSpec((tm, 128), lambda i: (i, 0))],
        out_shape=[jax.ShapeDtypeStruct((T, 128), f32)] * 2,
        name="rope_tables",
    )(positions.reshape(T, 1), inv_p, sgn)


def _front_kernel(x_ref, cos_ref, sin_ref, gmix_ref, wnat_ref, wtr_ref, convw_ref, convb_ref,
                  wqt_ref, wk_ref, gbias_ref, mog_ref, cqg_ref, ckvg_ref,
                  wuq_ref, wukvk_ref, wukvvt_ref, gq_ref, gk_ref,
                  ym_ref, q_ref, k_ref, vt_ref,
                  ubuf_ref, st_ref, mst_ref, *, tm):
    L = M_CHUNK

    @pl.when(pl.program_id(1) == 0)
    def _():
        ubuf_ref[0:8, :] = jnp.zeros((8, M_WIDTH), f32)
        st_ref[...] = jnp.zeros(st_ref.shape, f32)
        mst_ref[...] = jnp.zeros(mst_ref.shape, f32)

    srow = lax.broadcasted_iota(jnp.int32, (L, L), 0)
    qcol = lax.broadcasted_iota(jnp.int32, (L, L), 1)
    causal_t = srow <= qcol
    triu = jnp.where(causal_t, 1.0, 0.0).astype(bf16)
    ones_f = jnp.ones((ONES_ROWS, L), f32)
    ones_b = jnp.ones((ONES_ROWS, L), bf16)
    zpad = jnp.zeros((L - 8, L), f32)
    gq_n, gq_r = gq_ref[:, 0:128], gq_ref[:, 128:256]
    gk_n, gk_r = gk_ref[:, 0:128], gk_ref[:, 128:256]
    qscale = A_QK ** -0.5 * LOG2E
    m_run = mst_ref[:, 0:1]

    for sub in range(tm // SUB):
        rows = slice(sub * SUB, (sub + 1) * SUB)

        h = _rms(x_ref[rows, :], gmix_ref[...]).astype(bf16)
        pn = _dot(h, wnat_ref[...])
        pt = _dot_nt(wtr_ref[...], h)

        u = pn[:, N_U:N_U + M_WIDTH]
        ubuf_ref[8:8 + SUB, :] = u
        acc = convb_ref[...] + convw_ref[CONV_K - 1:CONV_K, :] * u
        for j in range(CONV_K - 1):
            off = 8 - (CONV_K - 1) + j
            acc = acc + convw_ref[j:j + 1, :] * ubuf_ref[off:off + SUB, :]
        ubuf_ref[0:8, :] = u[SUB - 8:SUB, :]
        ucb = (acc * _sigmoid(acc)).astype(bf16)

        cos = cos_ref[rows, :]
        sin = sin_ref[rows, :]

        def rope(t):
            return t * cos + pltpu.roll(t, 64, 1) * sin

        cqn = _rms(pn[:, N_CQ:N_CQ + Q_LORA], cqg_ref[...]).astype(bf16)
        qa = _dot(cqn, wuq_ref[...])
        ckvn = _rms(pn[:, N_CKV:N_CKV + KV_LORA], ckvg_ref[...]).astype(bf16)
        kn = _dot(ckvn, wukvk_ref[...])
        vt_ref[sub * A_HEADS * A_DV:(sub + 1) * A_HEADS * A_DV, :] = (
            _dot_nt(wukvvt_ref[...], ckvn).astype(bf16))
        kr = rope(_rms(pn[:, N_KR:N_KR + 128], gk_r, ROPE_DIM)).astype(bf16)
        for hh in range(A_HEADS):
            c0 = hh * QK_PAD
            qn = _rms(qa[:, c0:c0 + 128], gq_n) * qscale
            qr = rope(_rms(qa[:, c0 + 128:c0 + 256], gq_r, ROPE_DIM)) * qscale
            q_ref[rows, c0:c0 + 128] = qn.astype(bf16)
            q_ref[rows, c0 + 128:c0 + 256] = qr.astype(bf16)
            k_ref[rows, c0:c0 + 128] = _rms(kn[:, hh * 128:(hh + 1) * 128], gk_n).astype(bf16)
            k_ref[rows, c0 + 128:c0 + 256] = kr

        qt_all = _dot_nt(wqt_ref[...], ucb).astype(bf16)
        k_all = _dot(ucb, wk_ref[...]).astype(bf16)
        gi = pt[T_GI:T_GI + 8, :] + gbias_ref[0:8, :]
        logf = _log_sigmoid(pt[T_GF:T_GF + 8, :] + gbias_ref[8:16, :])
        for c in range(SUB // L):
            cols = slice(c * L, (c + 1) * L)
            crow = slice(sub * SUB + c * L, sub * SUB + (c + 1) * L)
            bsum = _dot(jnp.concatenate(_split3(logf[:, cols]), axis=0), triu)
            b = bsum[0:8] + bsum[8:16] + bsum[16:24]
            ic = gi[:, cols]
            btot = b[:, L - 1:L]
            a = btot - b + ic
            mloc = jnp.max(a, axis=1, keepdims=True)
            wa = jnp.exp(a - mloc)
            m_prev = m_run
            m_run = jnp.maximum(btot + m_prev, mloc)
            s_old = jnp.exp(btot + m_prev - m_run)
            s_loc = jnp.exp(mloc - m_run)
            r_t = jnp.concatenate([ic - b, zpad], axis=0).T
            inter_all = b + m_prev
            for hh in range(M_HEADS):
                hs = slice(hh * M_DV, (hh + 1) * M_DV)
                k_c = k_all[c * L:(c + 1) * L, hh * M_DKP:(hh + 1) * M_DKP]
                qt_c = qt_all[hh * M_DKP:(hh + 1) * M_DKP, cols]
                vt_f = jnp.concatenate([pt[T_V + hh * M_DV:T_V + (hh + 1) * M_DV, cols], ones_f], axis=0)
                vt_b = jnp.concatenate(
                    [pt[T_V + hh * M_DV:T_V + (hh + 1) * M_DV, cols].astype(bf16), ones_b], axis=0)
                st = st_ref[hh]

                dlog = jnp.where(causal_t, r_t[:, hh:hh + 1] + b[hh:hh + 1, :], -jnp.inf)
                inter = inter_all[hh:hh + 1, :]
                mj = jnp.maximum(inter, jnp.max(dlog, axis=0, keepdims=True))
                qk = (_dot(k_c, qt_c) * jnp.exp(dlog - mj)).astype(bf16)
                s_int = jnp.exp(inter - mj)
                rhs = jnp.concatenate([(qt_c.astype(f32) * s_int).astype(bf16), qk], axis=0)
                lhs = jnp.concatenate([st.astype(bf16), vt_b], axis=1)
                num = _dot(lhs, rhs)
                den = num[M_DV:M_DV + 1, :]
                hm = num[0:M_DV, :] / jnp.maximum(jnp.abs(den), jnp.exp(-mj))
                hn = hm * lax.rsqrt(jnp.sum(hm * hm, axis=0, keepdims=True) * (1.0 / M_DV) + EPS)
                y = hn.T * mog_ref[:, hs] * _sigmoid(pn[c * L:(c + 1) * L, N_O + hh * M_DV:N_O + (hh + 1) * M_DV])
                ym_ref[crow, hs] = y.astype(bf16)

                loc = _dot((vt_f * wa[hh:hh + 1, :]).astype(bf16), k_c)
                st_ref[hh] = s_old[hh:hh + 1, :] * st + s_loc[hh:hh + 1, :] * loc

    mst_ref[...] = jnp.broadcast_to(m_run, (8, 128))


def _front(x, cos, sin, w, B, S, tm):
    T = B * S
    nt = S // tm
    vt_rows = (tm // SUB) * A_HEADS * A_DV
    row = lambda b, i: (b * nt + i, 0)
    c2 = lambda b, i: (0, 0)

    def full(a):
        return pl.BlockSpec(a.shape, c2)

    weights = [w["gmix"], w["wnat"], w["wtr"], w["convw"], w["convb"], w["wqt"], w["wk"], w["gbias"],
               w["mog"], w["cqg"], w["ckvg"], w["wuq"], w["wukvk"], w["wukvvt"], w["gq"], w["gk"]]
    return pl.pallas_call(
        functools.partial(_front_kernel, tm=tm),
        grid=(B, nt),
        in_specs=[pl.BlockSpec((tm, D_MODEL), row),
                  pl.BlockSpec((tm, 128), row),
                  pl.BlockSpec((tm, 128), row)] + [full(a) for a in weights],
        out_specs=[pl.BlockSpec((tm, M_WIDTH), row),
                   pl.BlockSpec((tm, A_HEADS * QK_PAD), row),
                   pl.BlockSpec((tm, A_HEADS * QK_PAD), row),
                   pl.BlockSpec((vt_rows, SUB), row)],
        out_shape=[jax.ShapeDtypeStruct((T, M_WIDTH), bf16),
                   jax.ShapeDtypeStruct((T, A_HEADS * QK_PAD), bf16),
                   jax.ShapeDtypeStruct((T, A_HEADS * QK_PAD), bf16),
                   jax.ShapeDtypeStruct((B * nt * vt_rows, SUB), bf16)],
        scratch_shapes=[pltpu.VMEM((SUB + 8, M_WIDTH), f32),
                        pltpu.VMEM((M_HEADS, M_DV + ONES_ROWS, M_DKP), f32),
                        pltpu.VMEM((8, 128), f32)],
        compiler_params=pltpu.CompilerParams(
            dimension_semantics=("arbitrary", "arbitrary"), vmem_limit_bytes=VMEM_LIMIT),
        name="front",
    )(x, cos, sin, *weights)


def _attn_kernel(q_ref, k_ref, vt_ref, g_ref, o_ref, acc_ref, *, tq):
    qi = pl.program_id(1)
    ones = jnp.ones((ONES_ROWS, tq), bf16)
    acc_ref[...] = jnp.zeros(acc_ref.shape, f32)

    def block(j, ms, masked):
        r0 = pl.multiple_of(j * tq, tq)
        v0 = j * (A_HEADS * A_DV)
        out = []
        for hh in range(A_HEADS):
            c0 = hh * QK_PAD
            s = _dot_nt(k_ref[pl.ds(r0, tq), c0:c0 + QK_PAD], q_ref[:, c0:c0 + QK_PAD])
            if masked:
                row = lax.broadcasted_iota(jnp.int32, (tq, tq), 0)
                col = lax.broadcasted_iota(jnp.int32, (tq, tq), 1)
                s = jnp.where(row <= col, s, -jnp.inf)
            m_new = jnp.maximum(ms[hh], jnp.max(s, axis=0, keepdims=True))
            alpha = jnp.exp2(ms[hh] - m_new)
            p = jnp.exp2(s - m_new).astype(bf16)
            vte = jnp.concatenate(
                [vt_ref[pl.ds(pl.multiple_of(v0 + hh * A_DV, A_DV), A_DV), :], ones], axis=0)
            acc_ref[hh] = alpha * acc_ref[hh] + _dot(vte, p)
            out.append(m_new)
        return tuple(out)

    ms = tuple(jnp.full((1, tq), -jnp.inf, f32) for _ in range(A_HEADS))
    ms = lax.fori_loop(0, qi, lambda j, c: block(j, c, False), ms)
    block(qi, ms, True)
    for hh in range(A_HEADS):
        a = acc_ref[hh]
        o = a[0:A_DV, :] / a[A_DV:A_DV + 1, :]
        ms2 = jnp.sum(o * o, axis=0, keepdims=True) * (1.0 / A_DV)
        y = o * lax.rsqrt(ms2 + EPS) * g_ref[hh * A_DV:(hh + 1) * A_DV, :]
        o_ref[:, hh * A_DV:(hh + 1) * A_DV] = y.T.astype(bf16)


def _attention(q, k, vt, g, B, S, tq):
    T = B * S
    nq = S // tq
    return pl.pallas_call(
        functools.partial(_attn_kernel, tq=tq),
        grid=(B, nq),
        in_specs=[pl.BlockSpec((tq, A_HEADS * QK_PAD), lambda b, i: (b * nq + i, 0)),
                  pl.BlockSpec((S, A_HEADS * QK_PAD), lambda b, i: (b, 0)),
                  pl.BlockSpec((nq * A_HEADS * A_DV, tq), lambda b, i: (b, 0)),
                  pl.BlockSpec((A_HEADS * A_DV, 1), lambda b, i: (0, 0))],
        out_specs=pl.BlockSpec((tq, A_HEADS * A_DV), lambda b, i: (b * nq + i, 0)),
        out_shape=jax.ShapeDtypeStruct((T, A_HEADS * A_DV), bf16),
        scratch_shapes=[pltpu.VMEM((A_HEADS, A_DV + ONES_ROWS, tq), f32)],
        compiler_params=pltpu.CompilerParams(
            dimension_semantics=("arbitrary", "arbitrary"), vmem_limit_bytes=VMEM_LIMIT),
        name="mla_attention",
    )(q, k, vt, g)


def _memkv_kernel(mem_ref, g_ref, w_ref, kg_ref, k_ref, v_ref):
    mn = _rms(mem_ref[...], g_ref[...]).astype(bf16)
    kv = _dot(mn, w_ref[...])
    for hh in range(X_HEADS):
        sl = slice(hh * X_HD, (hh + 1) * X_HD)
        k_ref[:, sl] = _rms(kv[:, sl], kg_ref[...]).astype(bf16)
    v_ref[...] = kv[:, X_WIDTH:].astype(bf16)


def _memkv(mem2d, g, wkv, kg, B, Nm):
    return pl.pallas_call(
        _memkv_kernel,
        grid=(B,),
        in_specs=[pl.BlockSpec((Nm, D_MODEL), lambda b: (b, 0)),
                  pl.BlockSpec((1, D_MODEL), lambda b: (0, 0)),
                  pl.BlockSpec((D_MODEL, 2 * X_WIDTH), lambda b: (0, 0)),
                  pl.BlockSpec((1, X_HD), lambda b: (0, 0))],
        out_specs=[pl.BlockSpec((Nm, X_WIDTH), lambda b: (b, 0)),
                   pl.BlockSpec((Nm, X_WIDTH), lambda b: (b, 0))],
        out_shape=[jax.ShapeDtypeStruct((B * Nm, X_WIDTH), bf16)] * 2,
        compiler_params=pltpu.CompilerParams(
            dimension_semantics=("arbitrary",), vmem_limit_bytes=VMEM_LIMIT),
        name="mem_kv",
    )(mem2d, g, wkv, kg)


def _back_kernel(x_ref, ym_ref, ya_ref, wout_ref, gx_ref, wqx_ref, xqg_ref, kx_ref, vx_ref,
                 wox_ref, gf_ref, w1_ref, w2_ref, out_ref):
    x1 = (x_ref[...] + _dot(ym_ref[...], wout_ref[0:M_WIDTH, :])
          + _dot(ya_ref[...], wout_ref[M_WIDTH:2 * M_WIDTH, :]))

    hn = _rms(x1, gx_ref[...]).astype(bf16)
    qx = _dot(hn, wqx_ref[...])
    heads = []
    for hh in range(X_HEADS):
        sl = slice(hh * X_HD, (hh + 1) * X_HD)
        qh = (_rms(qx[:, sl], xqg_ref[...]) * (X_HD ** -0.5)).astype(bf16)
        s = _dot_nt(qh, kx_ref[:, sl])
        p = jnp.exp(s - jnp.max(s, axis=-1, keepdims=True))
        o = _dot(p.astype(bf16), vx_ref[:, sl]) / jnp.sum(p, axis=-1, keepdims=True)
        heads.append(o.astype(bf16))
    x2 = x1 + _dot(jnp.concatenate(heads, axis=-1), wox_ref[...])

    hf = _rms(x2, gf_ref[...]).astype(bf16)
    acc = x2
    for c in range(D_FF // FF_CHUNK):
        sl = slice(c * FF_CHUNK, (c + 1) * FF_CHUNK)
        a = jnp.maximum(_dot(hf, w1_ref[:, sl]), 0.0)
        acc = acc + _dot((a * a).astype(bf16), w2_ref[sl, :])
    out_ref[...] = acc


def _back(x, ym, ya, kx, vx, w, B, S, Nm, tm):
    T = B * S
    row = lambda i: (i, 0)
    c2 = lambda i: (0, 0)
    per_b = lambda i: ((i * tm) // S, 0)

    def const(a):
        return pl.BlockSpec(a.shape, c2, pipeline_mode=pl.Buffered(1))

    return pl.pallas_call(
        _back_kernel,
        grid=(T // tm,),
        in_specs=[pl.BlockSpec((tm, D_MODEL), row),
                  pl.BlockSpec((tm, M_WIDTH), row),
                  pl.BlockSpec((tm, M_WIDTH), row),
                  const(w["wout"]), const(w["gx"]), const(w["wqx"]), const(w["xqg"]),
                  pl.BlockSpec((Nm, X_WIDTH), per_b),
                  pl.BlockSpec((Nm, X_WIDTH), per_b),
                  const(w["wox"]), const(w["gf"]), const(w["w1"]), const(w["w2"])],
        out_specs=pl.BlockSpec((tm, D_MODEL), row),
        out_shape=jax.ShapeDtypeStruct((T, D_MODEL), f32),
        compiler_params=pltpu.CompilerParams(
            dimension_semantics=("arbitrary",), vmem_limit_bytes=VMEM_LIMIT),
        name="back",
    )(x, ym, ya, w["wout"], w["gx"], w["wqx"], w["xqg"], kx, vx, w["wox"], w["gf"], w["w1"], w["w2"])


def _rope_pad(a):
    z = jnp.zeros(a.shape[:-1] + (ROPE_DIM // 2,), a.dtype)
    return jnp.concatenate([a[..., :32], z, a[..., 32:], z], axis=-1)


def _block_diag_heads(w):
    H = w.shape[0]
    wp = jnp.pad(w, ((0, 0), (0, 0), (0, M_DKP - M_DK)))
    out = jnp.zeros((H, M_DV, H, M_DKP), w.dtype)
    for hh in range(H):
        out = out.at[hh, :, hh, :].set(wp[hh])
    return out.reshape(H * M_DV, H * M_DKP)


def _layer_weights(l, norm_mix_g, w_in, conv_w, conv_b, wq_m, wk_m, b_igate, b_fgate, m_out_g,
                   cq_norm_g, ckv_norm_g, w_uq, w_ukv, qk_norm_q, qk_norm_k, a_out_g, w_out,
                   norm_x_g, norm_mem_g, wq_x, wkv_x, xq_norm_g, xk_norm_g, wo_x,
                   norm_ffn_g, w_ff1, w_ff2):
    wi = w_in[l]
    o_v, o_o, o_g = M_WIDTH, 2 * M_WIDTH, 3 * M_WIDTH
    o_f = o_g + M_HEADS
    o_cq = o_f + M_HEADS
    o_ckv = o_cq + Q_LORA
    o_kr = o_ckv + KV_LORA
    wnat = jnp.concatenate([wi[:, :o_v], wi[:, o_o:o_g], wi[:, o_cq:o_kr], _rope_pad(wi[:, o_kr:])], axis=1)
    zg = jnp.zeros((8 - M_HEADS, D_MODEL), wi.dtype)
    wtr = jnp.concatenate([wi[:, o_v:o_o].T, wi[:, o_g:o_f].T, zg, wi[:, o_f:o_cq].T, zg], axis=0)
    z4 = jnp.zeros((8 - M_HEADS,), f32)
    gbias = jnp.concatenate([b_igate[l], z4, b_fgate[l], z4])[:, None]
    wuq = w_uq[l].reshape(Q_LORA, A_HEADS, A_QK)
    wuq = jnp.concatenate([wuq[..., :NOPE_DIM], _rope_pad(wuq[..., NOPE_DIM:])], axis=-1)
    wukv = w_ukv[l].reshape(KV_LORA, A_HEADS, NOPE_DIM + A_DV)
    return {
        "gmix": norm_mix_g[l][None, :],
        "wnat": wnat.astype(bf16),
        "wtr": wtr.astype(bf16),
        "convw": conv_w[l],
        "convb": conv_b[l][None, :],
        "wqt": (_block_diag_heads(wq_m[l]) * (M_DK ** -0.5)).T.astype(bf16),
        "wk": _block_diag_heads(wk_m[l]).astype(bf16),
        "gbias": gbias,
        "mog": m_out_g[l].reshape(1, M_WIDTH),
        "cqg": cq_norm_g[l][None, :],
        "ckvg": ckv_norm_g[l][None, :],
        "wuq": wuq.reshape(Q_LORA, A_HEADS * QK_PAD).astype(bf16),
        "wukvk": wukv[..., :NOPE_DIM].reshape(KV_LORA, A_HEADS * NOPE_DIM).astype(bf16),
        "wukvvt": wukv[..., NOPE_DIM:].reshape(KV_LORA, A_HEADS * A_DV).T.astype(bf16),
        "gq": jnp.concatenate([qk_norm_q[l][:NOPE_DIM], _rope_pad(qk_norm_q[l][NOPE_DIM:])])[None, :],
        "gk": jnp.concatenate([qk_norm_k[l][:NOPE_DIM], _rope_pad(qk_norm_k[l][NOPE_DIM:])])[None, :],
        "aog": a_out_g[l].reshape(A_HEADS * A_DV, 1),
        "wout": w_out[l].astype(bf16),
        "gx": norm_x_g[l][None, :],
        "gmem": norm_mem_g[l][None, :],
        "wqx": wq_x[l].astype(bf16),
        "wkvx": wkv_x[l].astype(bf16),
        "xqg": xq_norm_g[l][None, :],
        "xkg": xk_norm_g[l][None, :],
        "wox": wo_x[l].astype(bf16),
        "gf": norm_ffn_g[l][None, :],
        "w1": w_ff1[l].astype(bf16),
        "w2": w_ff2[l].astype(bf16),
    }


def kernel(x, mem, positions, norm_mix_g, w_in, conv_w, conv_b, wq_m, wk_m, b_igate, b_fgate, m_out_g, cq_norm_g, ckv_norm_g, w_uq, w_ukv, qk_norm_q, qk_norm_k, a_out_g, w_out, norm_x_g, norm_mem_g, wq_x, wkv_x, xq_norm_g, xk_norm_g, wo_x, norm_ffn_g, w_ff1, w_ff2):
    B, S, D = x.shape
    Nm = mem.shape[1]
    depth = w_in.shape[0]
    assert D == D_MODEL and S % SUB == 0
    tm_front = min(S, 512)
    tm_back = min(S, 512)
    params = (norm_mix_g, w_in, conv_w, conv_b, wq_m, wk_m, b_igate, b_fgate, m_out_g,
              cq_norm_g, ckv_norm_g, w_uq, w_ukv, qk_norm_q, qk_norm_k, a_out_g, w_out,
              norm_x_g, norm_mem_g, wq_x, wkv_x, xq_norm_g, xk_norm_g, wo_x,
              norm_ffn_g, w_ff1, w_ff2)

    cos, sin = _rope_tables(positions)
    xt = x.reshape(B * S, D)
    mem2d = mem.reshape(B * Nm, D)
    for l in range(depth):
        w = _layer_weights(l, *params)
        ym, q, k, vt = _front(xt, cos, sin, w, B, S, tm_front)
        ya = _attention(q, k, vt, w["aog"], B, S, SUB)
        kx, vx = _memkv(mem2d, w["gmem"], w["wkvx"], w["xkg"], B, Nm)
        xt = _back(xt, ym, ya, kx, vx, w, B, S, Nm, tm_back)
    return xt.reshape(B, S, D)
```

```python
import functools

import jax
import jax.numpy as jnp
from jax import lax
from jax.experimental import pallas as pl
from jax.experimental.pallas import tpu as pltpu

f32 = jnp.float32
bf16 = jnp.bfloat16

EPS = 1e-6
D_MODEL = 1024
M_HEADS = 4
M_WIDTH = 512
M_DV = 128
M_DK = 64
CONV_K = 4
A_HEADS = 4
A_DV = 128
Q_LORA = 256
KV_LORA = 128
NOPE_DIM = 128
ROPE_DIM = 64
A_QK = NOPE_DIM + ROPE_DIM
ROPE_THETA = 10000.0
X_HEADS = 4
X_HD = 128
X_WIDTH = 512
D_FF = 4096
FF_CHUNK = 1024

M_CHUNK = 128
M_DKP = 128
SUB = 256
QK_PAD = 256
ONES_ROWS = 16
N_U, N_O, N_CQ, N_CKV, N_KR, N_END = 0, 512, 1024, 1280, 1408, 1536
T_V, T_GI, T_GF, T_END = 0, 512, 520, 528

VMEM_LIMIT = 56 * 1024 * 1024
LOG2E = 1.4426950408889634


def _dot(a, b):
    return jnp.dot(a, b, preferred_element_type=f32)


def _dot_nt(a, b):
    return lax.dot_general(a, b, (((1,), (1,)), ((), ())), preferred_element_type=f32)


def _rms(x, g, n=None):
    n = x.shape[-1] if n is None else n
    ms = jnp.sum(x * x, axis=-1, keepdims=True) * (1.0 / n)
    return x * lax.rsqrt(ms + EPS) * g


def _sigmoid(x):
    return 1.0 / (1.0 + jnp.exp(-x))


def _log_sigmoid(x):
    return jnp.minimum(x, 0.0) - jnp.log1p(jnp.exp(-jnp.abs(x)))


def _split3(x):
    a = x.astype(bf16)
    r = x - a.astype(f32)
    b = r.astype(bf16)
    c = (r - b.astype(f32)).astype(bf16)
    return a, b, c


def _rope_kernel(pos_ref, inv_ref, sgn_ref, cos_ref, sin_ref):
    ang = pos_ref[...].astype(f32) * inv_ref[...]
    cos_ref[...] = jnp.cos(ang) * jnp.abs(sgn_ref[...])
    sin_ref[...] = jnp.sin(ang) * sgn_ref[...]


def _rope_tables(positions):
    T = positions.size
    tm = min(T, 2048)
    inv = 1.0 / (ROPE_THETA ** (jnp.arange(0, ROPE_DIM, 2, dtype=f32) / ROPE_DIM))
    z = jnp.zeros((ROPE_DIM // 2,), f32)
    o = jnp.ones((ROPE_DIM // 2,), f32)
    inv_p = jnp.concatenate([inv, z, inv, z])[None, :]
    sgn = jnp.concatenate([-o, z, o, z])[None, :]
    return pl.pallas_call(
        _rope_kernel,
        grid=(T // tm,),
        in_specs=[pl.BlockSpec((tm, 1), lambda i: (i, 0)),
                  pl.BlockSpec((1, 128), lambda i: (0, 0)),
                  pl.BlockSpec((1, 128), lambda i: (0, 0))],
        out_specs=[pl.BlockSpec((tm, 128), lambda i: (i, 0)),
                   pl.BlockSpec((tm, 128), lambda i: (i, 0))],
        out_shape=[jax.ShapeDtypeStruct((T, 128), f32)] * 2,
        name="rope_tables",
    )(positions.reshape(T, 1), inv_p, sgn)


def _front_kernel(x_ref, cos_ref, sin_ref, gmix_ref, wnat_ref, wtr_ref, convw_ref, convb_ref,
                  wqt_ref, wk_ref, gbias_ref, mog_ref, cqg_ref, ckvg_ref,
                  wuq_ref, wukvk_ref, wukvvt_ref, gq_ref, gk_ref,
                  ym_ref, q_ref, k_ref, vt_ref,
                  ubuf_ref, st_ref, mst_ref, *, tm):
    L = M_CHUNK

    @pl.when(pl.program_id(1) == 0)
    def _():
        ubuf_ref[0:8, :] = jnp.zeros((8, M_WIDTH), f32)
        st_ref[...] = jnp.zeros(st_ref.shape, f32)
        mst_ref[...] = jnp.zeros(mst_ref.shape, f32)

    srow = lax.broadcasted_iota(jnp.int32, (L, L), 0)
    qcol = lax.broadcasted_iota(jnp.int32, (L, L), 1)
    causal_t = srow <= qcol
    triu = jnp.where(causal_t, 1.0, 0.0).astype(bf16)
    ones_f = jnp.ones((ONES_ROWS, L), f32)
    ones_b = jnp.ones((ONES_ROWS, L), bf16)
    zpad = jnp.zeros((L - 8, L), f32)
    gq_n, gq_r = gq_ref[:, 0:128], gq_ref[:, 128:256]
    gk_n, gk_r = gk_ref[:, 0:128], gk_ref[:, 128:256]
    qscale = A_QK ** -0.5 * LOG2E
    m_run = mst_ref[:, 0:1]

    def project(sub):
        h = _rms(x_ref[sub * SUB:(sub + 1) * SUB, :], gmix_ref[...]).astype(bf16)
        pn = _dot(h, wnat_ref[...])
        pt = _dot_nt(wtr_ref[...], h)
        return pn, pt

    def prepare(sub, pn, pt):
        rows = slice(sub * SUB, (sub + 1) * SUB)

        u = pn[:, N_U:N_U + M_WIDTH]
        ubuf_ref[8:8 + SUB, :] = u
        acc = convb_ref[...] + convw_ref[CONV_K - 1:CONV_K, :] * u
        for j in range(CONV_K - 1):
            off = 8 - (CONV_K - 1) + j
            acc = acc + convw_ref[j:j + 1, :] * ubuf_ref[off:off + SUB, :]
        ubuf_ref[0:8, :] = u[SUB - 8:SUB, :]
        ucb = (acc * _sigmoid(acc)).astype(bf16)

        cos = cos_ref[rows, :]
        sin = sin_ref[rows, :]

        def rope(t):
            return t * cos + pltpu.roll(t, 64, 1) * sin

        cqn = _rms(pn[:, N_CQ:N_CQ + Q_LORA], cqg_ref[...]).astype(bf16)
        qa = _dot(cqn, wuq_ref[...])
        ckvn = _rms(pn[:, N_CKV:N_CKV + KV_LORA], ckvg_ref[...]).astype(bf16)
        kn = _dot(ckvn, wukvk_ref[...])
        vt_ref[sub * A_HEADS * A_DV:(sub + 1) * A_HEADS * A_DV, :] = (
            _dot_nt(wukvvt_ref[...], ckvn).astype(bf16))
        kr = rope(_rms(pn[:, N_KR:N_KR + 128], gk_r, ROPE_DIM)).astype(bf16)
        for hh in range(A_HEADS):
            c0 = hh * QK_PAD
            qn = _rms(qa[:, c0:c0 + 128], gq_n) * qscale
            qr = rope(_rms(qa[:, c0 + 128:c0 + 256], gq_r, ROPE_DIM)) * qscale
            q_ref[rows, c0:c0 + 128] = qn.astype(bf16)
            q_ref[rows, c0 + 128:c0 + 256] = qr.astype(bf16)
            k_ref[rows, c0:c0 + 128] = _rms(kn[:, hh * 128:(hh + 1) * 128], gk_n).astype(bf16)
            k_ref[rows, c0 + 128:c0 + 256] = kr

        qt_all = _dot_nt(wqt_ref[...], ucb).astype(bf16)
        k_all = _dot(ucb, wk_ref[...]).astype(bf16)
        gi = pt[T_GI:T_GI + 8, :] + gbias_ref[0:8, :]
        logf = _log_sigmoid(pt[T_GF:T_GF + 8, :] + gbias_ref[8:16, :])
        return pn, pt, qt_all, k_all, gi, logf

    def recur_states(vals, m_run, states):
        pn, pt, qt_all, k_all, gi, logf = vals
        work = []
        for c in range(SUB // L):
            cols = slice(c * L, (c + 1) * L)
            bsum = _dot(jnp.concatenate(_split3(logf[:, cols]), axis=0), triu)
            b = bsum[0:8] + bsum[8:16] + bsum[16:24]
            ic = gi[:, cols]
            btot = b[:, L - 1:L]
            a = btot - b + ic
            mloc = jnp.max(a, axis=1, keepdims=True)
            wa = jnp.exp(a - mloc)
            m_prev = m_run
            m_run = jnp.maximum(btot + m_prev, mloc)
            s_old = jnp.exp(btot + m_prev - m_run)
            s_loc = jnp.exp(mloc - m_run)
            r_t = jnp.concatenate([ic - b, zpad], axis=0).T
            inter_all = b + m_prev
            for hh in range(M_HEADS):
                k_c = k_all[c * L:(c + 1) * L, hh * M_DKP:(hh + 1) * M_DKP]
                qt_c = qt_all[hh * M_DKP:(hh + 1) * M_DKP, cols]
                vt_c = pt[T_V + hh * M_DV:T_V + (hh + 1) * M_DV, cols]
                st = states[hh]
                sc = _dot(k_c, qt_c)
                vt_w = jnp.concatenate([vt_c, ones_f], axis=0) * wa[hh:hh + 1, :]
                loc = _dot(vt_w.astype(bf16), k_c)
                states[hh] = s_old[hh:hh + 1, :] * st + s_loc[hh:hh + 1, :] * loc
                work.append((c, hh, sc, st, qt_c, vt_c, r_t[:, hh:hh + 1], b[hh:hh + 1, :],
                             inter_all[hh:hh + 1, :]))
        return work, m_run

    def recur_outputs(sub, pn, work):
        for c, hh, sc, st, qt_c, vt_c, r_col, b_row, inter in work:
            hs = slice(hh * M_DV, (hh + 1) * M_DV)
            crow = slice(sub * SUB + c * L, sub * SUB + (c + 1) * L)
            dlog = jnp.where(causal_t, r_col + b_row, -jnp.inf)
            mj = jnp.maximum(inter, jnp.max(dlog, axis=0, keepdims=True))
            qk = (sc * jnp.exp(dlog - mj)).astype(bf16)
            s_int = jnp.exp(inter - mj)
            rhs = jnp.concatenate([(qt_c.astype(f32) * s_int).astype(bf16), qk], axis=0)
            lhs = jnp.concatenate(
                [st.astype(bf16), jnp.concatenate([vt_c.astype(bf16), ones_b], axis=0)], axis=1)
            num = _dot(lhs, rhs)
            den = num[M_DV:M_DV + 1, :]
            hm = num[0:M_DV, :] / jnp.maximum(jnp.abs(den), jnp.exp(-mj))
            hn = hm * lax.rsqrt(jnp.sum(hm * hm, axis=0, keepdims=True) * (1.0 / M_DV) + EPS)
            y = hn.T * mog_ref[:, hs] * _sigmoid(pn[c * L:(c + 1) * L, N_O + hh * M_DV:N_O + (hh + 1) * M_DV])
            ym_ref[crow, hs] = y.astype(bf16)

    states = [st_ref[hh] for hh in range(M_HEADS)]
    nsub = tm // SUB
    pn, pt = project(0)
    vals = prepare(0, pn, pt)
    for sub in range(nsub):
        cur = vals
        if sub + 1 < nsub:
            pn, pt = project(sub + 1)
            vals = prepare(sub + 1, pn, pt)
        work, m_run = recur_states(cur, m_run, states)
        recur_outputs(sub, cur[0], work)
    for hh in range(M_HEADS):
        st_ref[hh] = states[hh]
    mst_ref[...] = jnp.broadcast_to(m_run, (8, 128))


def _front(x, cos, sin, w, B, S, tm):
    T = B * S
    nt = S // tm
    vt_rows = (tm // SUB) * A_HEADS * A_DV
    row = lambda b, i: (b * nt + i, 0)
    c2 = lambda b, i: (0, 0)

    def full(a):
        return pl.BlockSpec(a.shape, c2)

    weights = [w["gmix"], w["wnat"], w["wtr"], w["convw"], w["convb"], w["wqt"], w["wk"], w["gbias"],
               w["mog"], w["cqg"], w["ckvg"], w["wuq"], w["wukvk"], w["wukvvt"], w["gq"], w["gk"]]
    return pl.pallas_call(
        functools.partial(_front_kernel, tm=tm),
        grid=(B, nt),
        in_specs=[pl.BlockSpec((tm, D_MODEL), row),
                  pl.BlockSpec((tm, 128), row),
                  pl.BlockSpec((tm, 128), row)] + [full(a) for a in weights],
        out_specs=[pl.BlockSpec((tm, M_WIDTH), row),
                   pl.BlockSpec((tm, A_HEADS * QK_PAD), row),
                   pl.BlockSpec((tm, A_HEADS * QK_PAD), row),
                   pl.BlockSpec((vt_rows, SUB), row)],
        out_shape=[jax.ShapeDtypeStruct((T, M_WIDTH), bf16),
                   jax.ShapeDtypeStruct((T, A_HEADS * QK_PAD), bf16),
                   jax.ShapeDtypeStruct((T, A_HEADS * QK_PAD), bf16),
                   jax.ShapeDtypeStruct((B * nt * vt_rows, SUB), bf16)],
        scratch_shapes=[pltpu.VMEM((SUB + 8, M_WIDTH), f32),
                        pltpu.VMEM((M_HEADS, M_DV + ONES_ROWS, M_DKP), f32),
                        pltpu.VMEM((8, 128), f32)],
        compiler_params=pltpu.CompilerParams(
            dimension_semantics=("arbitrary", "arbitrary"), vmem_limit_bytes=VMEM_LIMIT),
        name="front",
    )(x, cos, sin, *weights)


def _attn_kernel(q_ref, k_ref, vt_ref, g_ref, o_ref, acc_ref, qt_ref, s_ref, *, tq):
    qi = pl.program_id(1)
    ones = jnp.ones((ONES_ROWS, SUB), bf16)
    acc_ref[...] = jnp.zeros(acc_ref.shape, f32)
    for hh in range(A_HEADS):
        qt_ref[hh] = q_ref[:, hh * QK_PAD:(hh + 1) * QK_PAD].T

    def scores(j):
        r0 = pl.multiple_of(j * tq, tq)
        return [_dot(k_ref[pl.ds(r0, tq), hh * QK_PAD:(hh + 1) * QK_PAD], qt_ref[hh])
                for hh in range(A_HEADS)]

    def consume(j, slot, ms, masked):
        v0 = j * ((tq // SUB) * A_HEADS * A_DV)
        out = []
        for hh in range(A_HEADS):
            s = s_ref[slot, hh]
            if masked:
                row = lax.broadcasted_iota(jnp.int32, (tq, tq), 0)
                col = lax.broadcasted_iota(jnp.int32, (tq, tq), 1)
                s = jnp.where(row <= col, s, -jnp.inf)
            m_new = jnp.maximum(ms[hh], jnp.max(s, axis=0, keepdims=True))
            alpha = jnp.exp2(ms[hh] - m_new)
            p = jnp.exp2(s - m_new).astype(bf16)
            pv = None
            for c in range(tq // SUB):
                vrow = pl.multiple_of(v0 + (c * A_HEADS + hh) * A_DV, A_DV)
                vte = jnp.concatenate([vt_ref[pl.ds(vrow, A_DV), :], ones], axis=0)
                d = _dot(vte, p[c * SUB:(c + 1) * SUB, :])
                pv = d if pv is None else pv + d
            acc_ref[hh] = alpha * acc_ref[hh] + pv
            out.append(m_new)
        return tuple(out)

    def body(j, ms):
        slot = j % 2
        nxt = scores(j + 1)
        ms = consume(j, slot, ms, False)
        for hh in range(A_HEADS):
            s_ref[1 - slot, hh] = nxt[hh]
        return ms

    first = scores(0)
    for hh in range(A_HEADS):
        s_ref[0, hh] = first[hh]
    ms = tuple(jnp.full((1, tq), -jnp.inf, f32) for _ in range(A_HEADS))
    ms = lax.fori_loop(0, qi, body, ms)
    consume(qi, qi % 2, ms, True)
    for hh in range(A_HEADS):
        a = acc_ref[hh]
        o = a[0:A_DV, :] / a[A_DV:A_DV + 1, :]
        ms2 = jnp.sum(o * o, axis=0, keepdims=True) * (1.0 / A_DV)
        y = o * lax.rsqrt(ms2 + EPS) * g_ref[hh * A_DV:(hh + 1) * A_DV, :]
        o_ref[:, hh * A_DV:(hh + 1) * A_DV] = y.T.astype(bf16)


def _attention(q, k, vt, g, B, S, tq):
    T = B * S
    nq = S // tq
    return pl.pallas_call(
        functools.partial(_attn_kernel, tq=tq),
        grid=(B, nq),
        in_specs=[pl.BlockSpec((tq, A_HEADS * QK_PAD), lambda b, i: (b * nq + i, 0)),
                  pl.BlockSpec((S, A_HEADS * QK_PAD), lambda b, i: (b, 0)),
                  pl.BlockSpec(((S // SUB) * A_HEADS * A_DV, SUB), lambda b, i: (b, 0)),
                  pl.BlockSpec((A_HEADS * A_DV, 1), lambda b, i: (0, 0))],
        out_specs=pl.BlockSpec((tq, A_HEADS * A_DV), lambda b, i: (b * nq + i, 0)),
        out_shape=jax.ShapeDtypeStruct((T, A_HEADS * A_DV), bf16),
        scratch_shapes=[pltpu.VMEM((A_HEADS, A_DV + ONES_ROWS, tq), f32),
                        pltpu.VMEM((A_HEADS, QK_PAD, tq), bf16),
                        pltpu.VMEM((2, A_HEADS, tq, tq), f32)],
        compiler_params=pltpu.CompilerParams(
            dimension_semantics=("arbitrary", "arbitrary"), vmem_limit_bytes=VMEM_LIMIT),
        name="mla_attention",
    )(q, k, vt, g)


def _memkv_kernel(mem_ref, g_ref, w_ref, kg_ref, k_ref, v_ref):
    mn = _rms(mem_ref[...], g_ref[...]).astype(bf16)
    kv = _dot(mn, w_ref[...])
    for hh in range(X_HEADS):
        sl = slice(hh * X_HD, (hh + 1) * X_HD)
        k_ref[:, sl] = _rms(kv[:, sl], kg_ref[...]).astype(bf16)
    v_ref[...] = kv[:, X_WIDTH:].astype(bf16)


def _memkv(mem2d, g, wkv, kg, B, Nm):
    return pl.pallas_call(
        _memkv_kernel,
        grid=(B,),
        in_specs=[pl.BlockSpec((Nm, D_MODEL), lambda b: (b, 0)),
                  pl.BlockSpec((1, D_MODEL), lambda b: (0, 0)),
                  pl.BlockSpec((D_MODEL, 2 * X_WIDTH), lambda b: (0, 0)),
                  pl.BlockSpec((1, X_HD), lambda b: (0, 0))],
        out_specs=[pl.BlockSpec((Nm, X_WIDTH), lambda b: (b, 0)),
                   pl.BlockSpec((Nm, X_WIDTH), lambda b: (b, 0))],
        out_shape=[jax.ShapeDtypeStruct((B * Nm, X_WIDTH), bf16)] * 2,
        compiler_params=pltpu.CompilerParams(
            dimension_semantics=("arbitrary",), vmem_limit_bytes=VMEM_LIMIT),
        name="mem_kv",
    )(mem2d, g, wkv, kg)


def _back_kernel(x_ref, ym_ref, ya_ref, wout_ref, gx_ref, wqx_ref, xqg_ref, kx_ref, vx_ref,
                 wox_ref, gf_ref, w1_ref, w2_ref, out_ref):
    x1 = (x_ref[...] + _dot(ym_ref[...], wout_ref[0:M_WIDTH, :])
          + _dot(ya_ref[...], wout_ref[M_WIDTH:2 * M_WIDTH, :]))

    hn = _rms(x1, gx_ref[...]).astype(bf16)
    qx = _dot(hn, wqx_ref[...])
    heads = []
    for hh in range(X_HEADS):
        sl = slice(hh * X_HD, (hh + 1) * X_HD)
        qh = (_rms(qx[:, sl], xqg_ref[...]) * (X_HD ** -0.5)).astype(bf16)
        s = _dot_nt(qh, kx_ref[:, sl])
        p = jnp.exp(s - jnp.max(s, axis=-1, keepdims=True))
        o = _dot(p.astype(bf16), vx_ref[:, sl]) / jnp.sum(p, axis=-1, keepdims=True)
        heads.append(o.astype(bf16))
    x2 = x1 + _dot(jnp.concatenate(heads, axis=-1), wox_ref[...])

    hf = _rms(x2, gf_ref[...]).astype(bf16)
    acc = x2
    for c in range(D_FF // FF_CHUNK):
        sl = slice(c * FF_CHUNK, (c + 1) * FF_CHUNK)
        a = jnp.maximum(_dot(hf, w1_ref[:, sl]), 0.0)
        acc = acc + _dot((a * a).astype(bf16), w2_ref[sl, :])
    out_ref[...] = acc


def _back(x, ym, ya, kx, vx, w, B, S, Nm, tm):
    T = B * S
    row = lambda i: (i, 0)
    c2 = lambda i: (0, 0)
    per_b = lambda i: ((i * tm) // S, 0)

    def const(a):
        return pl.BlockSpec(a.shape, c2, pipeline_mode=pl.Buffered(1))

    return pl.pallas_call(
        _back_kernel,
        grid=(T // tm,),
        in_specs=[pl.BlockSpec((tm, D_MODEL), row),
                  pl.BlockSpec((tm, M_WIDTH), row),
                  pl.BlockSpec((tm, M_WIDTH), row),
                  const(w["wout"]), const(w["gx"]), const(w["wqx"]), const(w["xqg"]),
                  pl.BlockSpec((Nm, X_WIDTH), per_b),
                  pl.BlockSpec((Nm, X_WIDTH), per_b),
                  const(w["wox"]), const(w["gf"]), const(w["w1"]), const(w["w2"])],
        out_specs=pl.BlockSpec((tm, D_MODEL), row),
        out_shape=jax.ShapeDtypeStruct((T, D_MODEL), f32),
        compiler_params=pltpu.CompilerParams(
            dimension_semantics=("arbitrary",), vmem_limit_bytes=VMEM_LIMIT),
        name="back",
    )(x, ym, ya, w["wout"], w["gx"], w["wqx"], w["xqg"], kx, vx, w["wox"], w["gf"], w["w1"], w["w2"])


def _rope_pad(a):
    z = jnp.zeros(a.shape[:-1] + (ROPE_DIM // 2,), a.dtype)
    return jnp.concatenate([a[..., :32], z, a[..., 32:], z], axis=-1)


def _block_diag_heads(w):
    H = w.shape[0]
    wp = jnp.pad(w, ((0, 0), (0, 0), (0, M_DKP - M_DK)))
    out = jnp.zeros((H, M_DV, H, M_DKP), w.dtype)
    for hh in range(H):
        out = out.at[hh, :, hh, :].set(wp[hh])
    return out.reshape(H * M_DV, H * M_DKP)


def _layer_weights(l, norm_mix_g, w_in, conv_w, conv_b, wq_m, wk_m, b_igate, b_fgate, m_out_g,
                   cq_norm_g, ckv_norm_g, w_uq, w_ukv, qk_norm_q, qk_norm_k, a_out_g, w_out,
                   norm_x_g, norm_mem_g, wq_x, wkv_x, xq_norm_g, xk_norm_g, wo_x,
                   norm_ffn_g, w_ff1, w_ff2):
    wi = w_in[l]
    o_v, o_o, o_g = M_WIDTH, 2 * M_WIDTH, 3 * M_WIDTH
    o_f = o_g + M_HEADS
    o_cq = o_f + M_HEADS
    o_ckv = o_cq + Q_LORA
    o_kr = o_ckv + KV_LORA
    wnat = jnp.concatenate([wi[:, :o_v], wi[:, o_o:o_g], wi[:, o_cq:o_kr], _rope_pad(wi[:, o_kr:])], axis=1)
    zg = jnp.zeros((8 - M_HEADS, D_MODEL), wi.dtype)
    wtr = jnp.concatenate([wi[:, o_v:o_o].T, wi[:, o_g:o_f].T, zg, wi[:, o_f:o_cq].T, zg], axis=0)
    z4 = jnp.zeros((8 - M_HEADS,), f32)
    gbias = jnp.concatenate([b_igate[l], z4, b_fgate[l], z4])[:, None]
    wuq = w_uq[l].reshape(Q_LORA, A_HEADS, A_QK)
    wuq = jnp.concatenate([wuq[..., :NOPE_DIM], _rope_pad(wuq[..., NOPE_DIM:])], axis=-1)
    wukv = w_ukv[l].reshape(KV_LORA, A_HEADS, NOPE_DIM + A_DV)
    return {
        "gmix": norm_mix_g[l][None, :],
        "wnat": wnat.astype(bf16),
        "wtr": wtr.astype(bf16),
        "convw": conv_w[l],
        "convb": conv_b[l][None, :],
        "wqt": (_block_diag_heads(wq_m[l]) * (M_DK ** -0.5)).T.astype(bf16),
        "wk": _block_diag_heads(wk_m[l]).astype(bf16),
        "gbias": gbias,
        "mog": m_out_g[l].reshape(1, M_WIDTH),
        "cqg": cq_norm_g[l][None, :],
        "ckvg": ckv_norm_g[l][None, :],
        "wuq": wuq.reshape(Q_LORA, A_HEADS * QK_PAD).astype(bf16),
        "wukvk": wukv[..., :NOPE_DIM].reshape(KV_LORA, A_HEADS * NOPE_DIM).astype(bf16),
        "wukvvt": wukv[..., NOPE_DIM:].reshape(KV_LORA, A_HEADS * A_DV).T.astype(bf16),
        "gq": jnp.concatenate([qk_norm_q[l][:NOPE_DIM], _rope_pad(qk_norm_q[l][NOPE_DIM:])])[None, :],
        "gk": jnp.concatenate([qk_norm_k[l][:NOPE_DIM], _rope_pad(qk_norm_k[l][NOPE_DIM:])])[None, :],
        "aog": a_out_g[l].reshape(A_HEADS * A_DV, 1),
        "wout": w_out[l].astype(bf16),
        "gx": norm_x_g[l][None, :],
        "gmem": norm_mem_g[l][None, :],
        "wqx": wq_x[l].astype(bf16),
        "wkvx": wkv_x[l].astype(bf16),
        "xqg": xq_norm_g[l][None, :],
        "xkg": xk_norm_g[l][None, :],
        "wox": wo_x[l].astype(bf16),
        "gf": norm_ffn_g[l][None, :],
        "w1": w_ff1[l].astype(bf16),
        "w2": w_ff2[l].astype(bf16),
    }


def kernel(x, mem, positions, norm_mix_g, w_in, conv_w, conv_b, wq_m, wk_m, b_igate, b_fgate, m_out_g, cq_norm_g, ckv_norm_g, w_uq, w_ukv, qk_norm_q, qk_norm_k, a_out_g, w_out, norm_x_g, norm_mem_g, wq_x, wkv_x, xq_norm_g, xk_norm_g, wo_x, norm_ffn_g, w_ff1, w_ff2):
    B, S, D = x.shape
    Nm = mem.shape[1]
    depth = w_in.shape[0]
    assert D == D_MODEL and S % SUB == 0
    tm_front = min(S, 1024)
    tm_back = min(S, 512)
    params = (norm_mix_g, w_in, conv_w, conv_b, wq_m, wk_m, b_igate, b_fgate, m_out_g,
              cq_norm_g, ckv_norm_g, w_uq, w_ukv, qk_norm_q, qk_norm_k, a_out_g, w_out,
              norm_x_g, norm_mem_g, wq_x, wkv_x, xq_norm_g, xk_norm_g, wo_x,
              norm_ffn_g, w_ff1, w_ff2)

    cos, sin = _rope_tables(positions)
    xt = x.reshape(B * S, D)
    mem2d = mem.reshape(B * Nm, D)
    for l in range(depth):
        w = _layer_weights(l, *params)
        ym, q, k, vt = _front(xt, cos, sin, w, B, S, tm_front)
        ya = _attention(q, k, vt, w["aog"], B, S, SUB)
        kx, vx = _memkv(mem2d, w["gmem"], w["wkvx"], w["xkg"], B, Nm)
        xt = _back(xt, ym, ya, kx, vx, w, B, S, Nm, tm_back)
    return xt.reshape(B, S, D)
```

```python
import functools

import jax
import jax.numpy as jnp
from jax import lax
from jax.experimental import pallas as pl
from jax.experimental.pallas import tpu as pltpu

f32 = jnp.float32
bf16 = jnp.bfloat16

EPS = 1e-6
D_MODEL = 1024
M_HEADS = 4
M_WIDTH = 512
M_DV = 128
M_DK = 64
CONV_K = 4
A_HEADS = 4
A_DV = 128
Q_LORA = 256
KV_LORA = 128
NOPE_DIM = 128
ROPE_DIM = 64
A_QK = NOPE_DIM + ROPE_DIM
ROPE_THETA = 10000.0
X_HEADS = 4
X_HD = 128
X_WIDTH = 512
D_FF = 4096
FF_CHUNK = 1024

M_CHUNK = 128
M_DKP = 128
SUB = 256
QK_PAD = 256
ONES_ROWS = 16
N_U, N_O, N_CQ, N_CKV, N_KR, N_END = 0, 512, 1024, 1280, 1408, 1536
T_V, T_GI, T_GF, T_END = 0, 512, 520, 528

VMEM_LIMIT = 56 * 1024 * 1024
LOG2E = 1.4426950408889634


def _dot(a, b):
    return jnp.dot(a, b, preferred_element_type=f32)


def _dot_nt(a, b):
    return lax.dot_general(a, b, (((1,), (1,)), ((), ())), preferred_element_type=f32)


def _rms(x, g, n=None):
    n = x.shape[-1] if n is None else n
    ms = jnp.sum(x * x, axis=-1, keepdims=True) * (1.0 / n)
    return x * lax.rsqrt(ms + EPS) * g


def _sigmoid(x):
    return 1.0 / (1.0 + jnp.exp(-x))


def _log_sigmoid(x):
    return jnp.minimum(x, 0.0) - jnp.log1p(jnp.exp(-jnp.abs(x)))


def _split3(x):
    a = x.astype(bf16)
    r = x - a.astype(f32)
    b = r.astype(bf16)
    c = (r - b.astype(f32)).astype(bf16)
    return a, b, c


def _rope_kernel(pos_ref, inv_ref, sgn_ref, cos_ref, sin_ref):
    ang = pos_ref[...].astype(f32) * inv_ref[...]
    cos_ref[...] = jnp.cos(ang) * jnp.abs(sgn_ref[...])
    sin_ref[...] = jnp.sin(ang) * sgn_ref[...]


def _rope_tables(positions):
    T = positions.size
    tm = min(T, 2048)
    inv = 1.0 / (ROPE_THETA ** (jnp.arange(0, ROPE_DIM, 2, dtype=f32) / ROPE_DIM))
    z = jnp.zeros((ROPE_DIM // 2,), f32)
    o = jnp.ones((ROPE_DIM // 2,), f32)
    inv_p = jnp.concatenate([inv, z, inv, z])[None, :]
    sgn = jnp.concatenate([-o, z, o, z])[None, :]
    return pl.pallas_call(
        _rope_kernel,
        grid=(T // tm,),
        in_specs=[pl.BlockSpec((tm, 1), lambda i: (i, 0)),
                  pl.BlockSpec((1, 128), lambda i: (0, 0)),
                  pl.BlockSpec((1, 128), lambda i: (0, 0))],
        out_specs=[pl.BlockSpec((tm, 128), lambda i: (i, 0)),
                   pl.BlockSpec((tm, 128), lambda i: (i, 0))],
        out_shape=[jax.ShapeDtypeStruct((T, 128), f32)] * 2,
        name="rope_tables",
    )(positions.reshape(T, 1), inv_p, sgn)


def _front_kernel(x_ref, cos_ref, sin_ref, gmix_ref, wnat_ref, wtr_ref, convw_ref, convb_ref,
                  wqt_ref, wk_ref, gbias_ref, mog_ref, cqg_ref, ckvg_ref,
                  wuq_ref, wukvk_ref, wukvvt_ref, gq_ref, gk_ref,
                  ym_ref, q_ref, k_ref, vt_ref,
                  ubuf_ref, st_ref, mst_ref, *, tm):
    L = M_CHUNK

    @pl.when(pl.program_id(1) == 0)
    def _():
        ubuf_ref[0:8, :] = jnp.zeros((8, M_WIDTH), f32)
        st_ref[...] = jnp.zeros(st_ref.shape, f32)
        mst_ref[...] = jnp.zeros(mst_ref.shape, f32)

    srow = lax.broadcasted_iota(jnp.int32, (L, L), 0)
    qcol = lax.broadcasted_iota(jnp.int32, (L, L), 1)
    causal_t = srow <= qcol
    triu = jnp.where(causal_t, 1.0, 0.0).astype(bf16)
    ones_f = jnp.ones((ONES_ROWS, L), f32)
    ones_b = jnp.ones((ONES_ROWS, L), bf16)
    zpad = jnp.zeros((L - 8, L), f32)
    gq_n, gq_r = gq_ref[:, 0:128], gq_ref[:, 128:256]
    gk_n, gk_r = gk_ref[:, 0:128], gk_ref[:, 128:256]
    qscale = A_QK ** -0.5 * LOG2E
    m_run = mst_ref[:, 0:1]

    def project(sub):
        h = _rms(x_ref[sub * SUB:(sub + 1) * SUB, :], gmix_ref[...]).astype(bf16)
        pn = _dot(h, wnat_ref[...])
        pt = _dot_nt(wtr_ref[...], h)
        return pn, pt

    def prepare(sub, pn, pt):
        rows = slice(sub * SUB, (sub + 1) * SUB)

        u = pn[:, N_U:N_U + M_WIDTH]
        ubuf_ref[8:8 + SUB, :] = u
        acc = convb_ref[...] + convw_ref[CONV_K - 1:CONV_K, :] * u
        for j in range(CONV_K - 1):
            off = 8 - (CONV_K - 1) + j
            acc = acc + convw_ref[j:j + 1, :] * ubuf_ref[off:off + SUB, :]
        ubuf_ref[0:8, :] = u[SUB - 8:SUB, :]
        ucb = (acc * _sigmoid(acc)).astype(bf16)

        cos = cos_ref[rows, :]
        sin = sin_ref[rows, :]

        def rope(t):
            return t * cos + pltpu.roll(t, 64, 1) * sin

        cqn = _rms(pn[:, N_CQ:N_CQ + Q_LORA], cqg_ref[...]).astype(bf16)
        qa = _dot(cqn, wuq_ref[...])
        ckvn = _rms(pn[:, N_CKV:N_CKV + KV_LORA], ckvg_ref[...]).astype(bf16)
        kn = _dot(ckvn, wukvk_ref[...])
        vt_ref[sub * A_HEADS * A_DV:(sub + 1) * A_HEADS * A_DV, :] = (
            _dot_nt(wukvvt_ref[...], ckvn).astype(bf16))
        kr = rope(_rms(pn[:, N_KR:N_KR + 128], gk_r, ROPE_DIM)).astype(bf16)
        for hh in range(A_HEADS):
            c0 = hh * QK_PAD
            qn = _rms(qa[:, c0:c0 + 128], gq_n) * qscale
            qr = rope(_rms(qa[:, c0 + 128:c0 + 256], gq_r, ROPE_DIM)) * qscale
            q_ref[rows, c0:c0 + 128] = qn.astype(bf16)
            q_ref[rows, c0 + 128:c0 + 256] = qr.astype(bf16)
            k_ref[rows, c0:c0 + 128] = _rms(kn[:, hh * 128:(hh + 1) * 128], gk_n).astype(bf16)
            k_ref[rows, c0 + 128:c0 + 256] = kr

        qt_all = _dot_nt(wqt_ref[...], ucb).astype(bf16)
        k_all = _dot(ucb, wk_ref[...]).astype(bf16)
        gi = pt[T_GI:T_GI + 8, :] + gbias_ref[0:8, :]
        logf = _log_sigmoid(pt[T_GF:T_GF + 8, :] + gbias_ref[8:16, :])
        return pn, pt, qt_all, k_all, gi, logf

    def recur_states(vals, m_run, states):
        pn, pt, qt_all, k_all, gi, logf = vals
        work = []
        for c in range(SUB // L):
            cols = slice(c * L, (c + 1) * L)
            bsum = _dot(jnp.concatenate(_split3(logf[:, cols]), axis=0), triu)
            b = bsum[0:8] + bsum[8:16] + bsum[16:24]
            ic = gi[:, cols]
            btot = b[:, L - 1:L]
            a = btot - b + ic
            mloc = jnp.max(a, axis=1, keepdims=True)
            wa = jnp.exp(a - mloc)
            m_prev = m_run
            m_run = jnp.maximum(btot + m_prev, mloc)
            s_old = jnp.exp(btot + m_prev - m_run)
            s_loc = jnp.exp(mloc - m_run)
            r_t = jnp.concatenate([ic - b, zpad], axis=0).T
            inter_all = b + m_prev
            for hh in range(M_HEADS):
                k_c = k_all[c * L:(c + 1) * L, hh * M_DKP:(hh + 1) * M_DKP]
                qt_c = qt_all[hh * M_DKP:(hh + 1) * M_DKP, cols]
                vt_c = pt[T_V + hh * M_DV:T_V + (hh + 1) * M_DV, cols]
                st = states[hh]
                sc = _dot(k_c, qt_c)
                vt_w = jnp.concatenate([vt_c, ones_f], axis=0) * wa[hh:hh + 1, :]
                loc = _dot(vt_w.astype(bf16), k_c)
                states[hh] = s_old[hh:hh + 1, :] * st + s_loc[hh:hh + 1, :] * loc
                work.append((c, hh, sc, st, qt_c, vt_c, r_t[:, hh:hh + 1], b[hh:hh + 1, :],
                             inter_all[hh:hh + 1, :]))
        return work, m_run

    def recur_outputs(sub, pn, work):
        for c, hh, sc, st, qt_c, vt_c, r_col, b_row, inter in work:
            hs = slice(hh * M_DV, (hh + 1) * M_DV)
            crow = slice(sub * SUB + c * L, sub * SUB + (c + 1) * L)
            dlog = jnp.where(causal_t, r_col + b_row, -jnp.inf)
            mj = jnp.maximum(inter, jnp.max(dlog, axis=0, keepdims=True))
            qk = (sc * jnp.exp(dlog - mj)).astype(bf16)
            s_int = jnp.exp(inter - mj)
            rhs = jnp.concatenate([(qt_c.astype(f32) * s_int).astype(bf16), qk], axis=0)
            lhs = jnp.concatenate(
                [st.astype(bf16), jnp.concatenate([vt_c.astype(bf16), ones_b], axis=0)], axis=1)
            num = _dot(lhs, rhs)
            den = num[M_DV:M_DV + 1, :]
            hm = num[0:M_DV, :] / jnp.maximum(jnp.abs(den), jnp.exp(-mj))
            hn = hm * lax.rsqrt(jnp.sum(hm * hm, axis=0, keepdims=True) * (1.0 / M_DV) + EPS)
            y = hn.T * mog_ref[:, hs] * _sigmoid(pn[c * L:(c + 1) * L, N_O + hh * M_DV:N_O + (hh + 1) * M_DV])
            ym_ref[crow, hs] = y.astype(bf16)

    states = [st_ref[hh] for hh in range(M_HEADS)]
    nsub = tm // SUB
    pn, pt = project(0)
    vals = prepare(0, pn, pt)
    for sub in range(nsub):
        cur = vals
        if sub + 1 < nsub:
            pn, pt = project(sub + 1)
            vals = prepare(sub + 1, pn, pt)
        work, m_run = recur_states(cur, m_run, states)
        recur_outputs(sub, cur[0], work)
    for hh in range(M_HEADS):
        st_ref[hh] = states[hh]
    mst_ref[...] = jnp.broadcast_to(m_run, (8, 128))


def _front(x, cos, sin, w, B, S, tm):
    T = B * S
    nt = S // tm
    vt_rows = (tm // SUB) * A_HEADS * A_DV
    row = lambda b, i: (b * nt + i, 0)
    c2 = lambda b, i: (0, 0)

    def full(a):
        return pl.BlockSpec(a.shape, c2)

    weights = [w["gmix"], w["wnat"], w["wtr"], w["convw"], w["convb"], w["wqt"], w["wk"], w["gbias"],
               w["mog"], w["cqg"], w["ckvg"], w["wuq"], w["wukvk"], w["wukvvt"], w["gq"], w["gk"]]
    return pl.pallas_call(
        functools.partial(_front_kernel, tm=tm),
        grid=(B, nt),
        in_specs=[pl.BlockSpec((tm, D_MODEL), row),
                  pl.BlockSpec((tm, 128), row),
                  pl.BlockSpec((tm, 128), row)] + [full(a) for a in weights],
        out_specs=[pl.BlockSpec((tm, M_WIDTH), row),
                   pl.BlockSpec((tm, A_HEADS * QK_PAD), row),
                   pl.BlockSpec((tm, A_HEADS * QK_PAD), row),
                   pl.BlockSpec((vt_rows, SUB), row)],
        out_shape=[jax.ShapeDtypeStruct((T, M_WIDTH), bf16),
                   jax.ShapeDtypeStruct((T, A_HEADS * QK_PAD), bf16),
                   jax.ShapeDtypeStruct((T, A_HEADS * QK_PAD), bf16),
                   jax.ShapeDtypeStruct((B * nt * vt_rows, SUB), bf16)],
        scratch_shapes=[pltpu.VMEM((SUB + 8, M_WIDTH), f32),
                        pltpu.VMEM((M_HEADS, M_DV + ONES_ROWS, M_DKP), f32),
                        pltpu.VMEM((8, 128), f32)],
        compiler_params=pltpu.CompilerParams(
            dimension_semantics=("arbitrary", "arbitrary"), vmem_limit_bytes=VMEM_LIMIT),
        name="front",
    )(x, cos, sin, *weights)


def _attn_kernel(q_ref, k_ref, vt_ref, g_ref, o_ref, acc_ref, *, tq):
    qi = pl.program_id(1)
    ones = jnp.ones((ONES_ROWS, tq), bf16)
    acc_ref[...] = jnp.zeros(acc_ref.shape, f32)

    def block(j, ms, masked):
        r0 = pl.multiple_of(j * tq, tq)
        v0 = j * (A_HEADS * A_DV)
        out = []
        for hh in range(A_HEADS):
            c0 = hh * QK_PAD
            s = _dot_nt(k_ref[pl.ds(r0, tq), c0:c0 + QK_PAD], q_ref[:, c0:c0 + QK_PAD])
            if masked:
                row = lax.broadcasted_iota(jnp.int32, (tq, tq), 0)
                col = lax.broadcasted_iota(jnp.int32, (tq, tq), 1)
                s = jnp.where(row <= col, s, -jnp.inf)
            m_new = jnp.maximum(ms[hh], jnp.max(s, axis=0, keepdims=True))
            alpha = jnp.exp2(ms[hh] - m_new)
            p = jnp.exp2(s - m_new).astype(bf16)
            vte = jnp.concatenate(
                [vt_ref[pl.ds(pl.multiple_of(v0 + hh * A_DV, A_DV), A_DV), :], ones], axis=0)
            acc_ref[hh] = alpha * acc_ref[hh] + _dot(vte, p)
            out.append(m_new)
        return tuple(out)

    ms = tuple(jnp.full((1, tq), -jnp.inf, f32) for _ in range(A_HEADS))
    ms = lax.fori_loop(0, qi, lambda j, c: block(j, c, False), ms)
    block(qi, ms, True)
    for hh in range(A_HEADS):
        a = acc_ref[hh]
        o = a[0:A_DV, :] / a[A_DV:A_DV + 1, :]
        ms2 = jnp.sum(o * o, axis=0, keepdims=True) * (1.0 / A_DV)
        y = o * lax.rsqrt(ms2 + EPS) * g_ref[hh * A_DV:(hh + 1) * A_DV, :]
        o_ref[:, hh * A_DV:(hh + 1) * A_DV] = y.T.astype(bf16)


def _attention(q, k, vt, g, B, S, tq):
    T = B * S
    nq = S // tq
    return pl.pallas_call(
        functools.partial(_attn_kernel, tq=tq),
        grid=(B, nq),
        in_specs=[pl.BlockSpec((tq, A_HEADS * QK_PAD), lambda b, i: (b * nq + i, 0)),
                  pl.BlockSpec((S, A_HEADS * QK_PAD), lambda b, i: (b, 0)),
                  pl.BlockSpec((nq * A_HEADS * A_DV, tq), lambda b, i: (b, 0)),
                  pl.BlockSpec((A_HEADS * A_DV, 1), lambda b, i: (0, 0))],
        out_specs=pl.BlockSpec((tq, A_HEADS * A_DV), lambda b, i: (b * nq + i, 0)),
        out_shape=jax.ShapeDtypeStruct((T, A_HEADS * A_DV), bf16),
        scratch_shapes=[pltpu.VMEM((A_HEADS, A_DV + ONES_ROWS, tq), f32)],
        compiler_params=pltpu.CompilerParams(
            dimension_semantics=("arbitrary", "arbitrary"), vmem_limit_bytes=VMEM_LIMIT),
        name="mla_attention",
    )(q, k, vt, g)


def _memkv_kernel(mem_ref, g_ref, w_ref, kg_ref, k_ref, v_ref):
    mn = _rms(mem_ref[...], g_ref[...]).astype(bf16)
    kv = _dot(mn, w_ref[...])
    for hh in range(X_HEADS):
        sl = slice(hh * X_HD, (hh + 1) * X_HD)
        k_ref[:, sl] = _rms(kv[:, sl], kg_ref[...]).astype(bf16)
    v_ref[...] = kv[:, X_WIDTH:].astype(bf16)


def _memkv(mem2d, g, wkv, kg, B, Nm):
    return pl.pallas_call(
        _memkv_kernel,
        grid=(B,),
        in_specs=[pl.BlockSpec((Nm, D_MODEL), lambda b: (b, 0)),
                  pl.BlockSpec((1, D_MODEL), lambda b: (0, 0)),
                  pl.BlockSpec((D_MODEL, 2 * X_WIDTH), lambda b: (0, 0)),
                  pl.BlockSpec((1, X_HD), lambda b: (0, 0))],
        out_specs=[pl.BlockSpec((Nm, X_WIDTH), lambda b: (b, 0)),
                   pl.BlockSpec((Nm, X_WIDTH), lambda b: (b, 0))],
        out_shape=[jax.ShapeDtypeStruct((B * Nm, X_WIDTH), bf16)] * 2,
        compiler_params=pltpu.CompilerParams(
            dimension_semantics=("arbitrary",), vmem_limit_bytes=VMEM_LIMIT),
        name="mem_kv",
    )(mem2d, g, wkv, kg)


def _back_kernel(x_ref, ym_ref, ya_ref, wout_ref, gx_ref, wqx_ref, xqg_ref, kx_ref, vx_ref,
                 wox_ref, gf_ref, w1_ref, w2_ref, out_ref):
    x1 = (x_ref[...] + _dot(ym_ref[...], wout_ref[0:M_WIDTH, :])
          + _dot(ya_ref[...], wout_ref[M_WIDTH:2 * M_WIDTH, :]))

    hn = _rms(x1, gx_ref[...]).astype(bf16)
    qx = _dot(hn, wqx_ref[...])
    heads = []
    for hh in range(X_HEADS):
        sl = slice(hh * X_HD, (hh + 1) * X_HD)
        qh = (_rms(qx[:, sl], xqg_ref[...]) * (X_HD ** -0.5)).astype(bf16)
        s = _dot_nt(qh, kx_ref[:, sl])
        p = jnp.exp(s - jnp.max(s, axis=-1, keepdims=True))
        o = _dot(p.astype(bf16), vx_ref[:, sl]) / jnp.sum(p, axis=-1, keepdims=True)
        heads.append(o.astype(bf16))
    x2 = x1 + _dot(jnp.concatenate(heads, axis=-1), wox_ref[...])

    hf = _rms(x2, gf_ref[...]).astype(bf16)
    acc = x2
    for c in range(D_FF // FF_CHUNK):
        sl = slice(c * FF_CHUNK, (c + 1) * FF_CHUNK)
        a = jnp.maximum(_dot(hf, w1_ref[:, sl]), 0.0)
        acc = acc + _dot((a * a).astype(bf16), w2_ref[sl, :])
    out_ref[...] = acc


def _back(x, ym, ya, kx, vx, w, B, S, Nm, tm):
    T = B * S
    row = lambda i: (i, 0)
    c2 = lambda i: (0, 0)
    per_b = lambda i: ((i * tm) // S, 0)

    def const(a):
        return pl.BlockSpec(a.shape, c2, pipeline_mode=pl.Buffered(1))

    return pl.pallas_call(
        _back_kernel,
        grid=(T // tm,),
        in_specs=[pl.BlockSpec((tm, D_MODEL), row),
                  pl.BlockSpec((tm, M_WIDTH), row),
                  pl.BlockSpec((tm, M_WIDTH), row),
                  const(w["wout"]), const(w["gx"]), const(w["wqx"]), const(w["xqg"]),
                  pl.BlockSpec((Nm, X_WIDTH), per_b),
                  pl.BlockSpec((Nm, X_WIDTH), per_b),
                  const(w["wox"]), const(w["gf"]), const(w["w1"]), const(w["w2"])],
        out_specs=pl.BlockSpec((tm, D_MODEL), row),
        out_shape=jax.ShapeDtypeStruct((T, D_MODEL), f32),
        compiler_params=pltpu.CompilerParams(
            dimension_semantics=("arbitrary",), vmem_limit_bytes=VMEM_LIMIT),
        name="back",
    )(x, ym, ya, w["wout"], w["gx"], w["wqx"], w["xqg"], kx, vx, w["wox"], w["gf"], w["w1"], w["w2"])


def _rope_pad(a):
    z = jnp.zeros(a.shape[:-1] + (ROPE_DIM // 2,), a.dtype)
    return jnp.concatenate([a[..., :32], z, a[..., 32:], z], axis=-1)


def _block_diag_heads(w):
    H = w.shape[0]
    wp = jnp.pad(w, ((0, 0), (0, 0), (0, M_DKP - M_DK)))
    out = jnp.zeros((H, M_DV, H, M_DKP), w.dtype)
    for hh in range(H):
        out = out.at[hh, :, hh, :].set(wp[hh])
    return out.reshape(H * M_DV, H * M_DKP)


def _layer_weights(l, norm_mix_g, w_in, conv_w, conv_b, wq_m, wk_m, b_igate, b_fgate, m_out_g,
                   cq_norm_g, ckv_norm_g, w_uq, w_ukv, qk_norm_q, qk_norm_k, a_out_g, w_out,
                   norm_x_g, norm_mem_g, wq_x, wkv_x, xq_norm_g, xk_norm_g, wo_x,
                   norm_ffn_g, w_ff1, w_ff2):
    wi = w_in[l]
    o_v, o_o, o_g = M_WIDTH, 2 * M_WIDTH, 3 * M_WIDTH
    o_f = o_g + M_HEADS
    o_cq = o_f + M_HEADS
    o_ckv = o_cq + Q_LORA
    o_kr = o_ckv + KV_LORA
    wnat = jnp.concatenate([wi[:, :o_v], wi[:, o_o:o_g], wi[:, o_cq:o_kr], _rope_pad(wi[:, o_kr:])], axis=1)
    zg = jnp.zeros((8 - M_HEADS, D_MODEL), wi.dtype)
    wtr = jnp.concatenate([wi[:, o_v:o_o].T, wi[:, o_g:o_f].T, zg, wi[:, o_f:o_cq].T, zg], axis=0)
    z4 = jnp.zeros((8 - M_HEADS,), f32)
    gbias = jnp.concatenate([b_igate[l], z4, b_fgate[l], z4])[:, None]
    wuq = w_uq[l].reshape(Q_LORA, A_HEADS, A_QK)
    wuq = jnp.concatenate([wuq[..., :NOPE_DIM], _rope_pad(wuq[..., NOPE_DIM:])], axis=-1)
    wukv = w_ukv[l].reshape(KV_LORA, A_HEADS, NOPE_DIM + A_DV)
    return {
        "gmix": norm_mix_g[l][None, :],
        "wnat": wnat.astype(bf16),
        "wtr": wtr.astype(bf16),
        "convw": conv_w[l],
        "convb": conv_b[l][None, :],
        "wqt": (_block_diag_heads(wq_m[l]) * (M_DK ** -0.5)).T.astype(bf16),
        "wk": _block_diag_heads(wk_m[l]).astype(bf16),
        "gbias": gbias,
        "mog": m_out_g[l].reshape(1, M_WIDTH),
        "cqg": cq_norm_g[l][None, :],
        "ckvg": ckv_norm_g[l][None, :],
        "wuq": wuq.reshape(Q_LORA, A_HEADS * QK_PAD).astype(bf16),
        "wukvk": wukv[..., :NOPE_DIM].reshape(KV_LORA, A_HEADS * NOPE_DIM).astype(bf16),
        "wukvvt": wukv[..., NOPE_DIM:].reshape(KV_LORA, A_HEADS * A_DV).T.astype(bf16),
        "gq": jnp.concatenate([qk_norm_q[l][:NOPE_DIM], _rope_pad(qk_norm_q[l][NOPE_DIM:])])[None, :],
        "gk": jnp.concatenate([qk_norm_k[l][:NOPE_DIM], _rope_pad(qk_norm_k[l][NOPE_DIM:])])[None, :],
        "aog": a_out_g[l].reshape(A_HEADS * A_DV, 1),
        "wout": w_out[l].astype(bf16),
        "gx": norm_x_g[l][None, :],
        "gmem": norm_mem_g[l][None, :],
        "wqx": wq_x[l].astype(bf16),
        "wkvx": wkv_x[l].astype(bf16),
        "xqg": xq_norm_g[l][None, :],
        "xkg": xk_norm_g[l][None, :],
        "wox": wo_x[l].astype(bf16),
        "gf": norm_ffn_g[l][None, :],
        "w1": w_ff1[l].astype(bf16),
        "w2": w_ff2[l].astype(bf16),
    }


def kernel(x, mem, positions, norm_mix_g, w_in, conv_w, conv_b, wq_m, wk_m, b_igate, b_fgate, m_out_g, cq_norm_g, ckv_norm_g, w_uq, w_ukv, qk_norm_q, qk_norm_k, a_out_g, w_out, norm_x_g, norm_mem_g, wq_x, wkv_x, xq_norm_g, xk_norm_g, wo_x, norm_ffn_g, w_ff1, w_ff2):
    B, S, D = x.shape
    Nm = mem.shape[1]
    depth = w_in.shape[0]
    assert D == D_MODEL and S % SUB == 0
    tm_front = min(S, 1024)
    tm_back = min(S, 512)
    params = (norm_mix_g, w_in, conv_w, conv_b, wq_m, wk_m, b_igate, b_fgate, m_out_g,
              cq_norm_g, ckv_norm_g, w_uq, w_ukv, qk_norm_q, qk_norm_k, a_out_g, w_out,
              norm_x_g, norm_mem_g, wq_x, wkv_x, xq_norm_g, xk_norm_g, wo_x,
              norm_ffn_g, w_ff1, w_ff2)

    cos, sin = _rope_tables(positions)
    xt = x.reshape(B * S, D)
    mem2d = mem.reshape(B * Nm, D)
    for l in range(depth):
        w = _layer_weights(l, *params)
        ym, q, k, vt = _front(xt, cos, sin, w, B, S, tm_front)
        ya = _attention(q, k, vt, w["aog"], B, S, SUB)
        kx, vx = _memkv(mem2d, w["gmem"], w["wkvx"], w["xkg"], B, Nm)
        xt = _back(xt, ym, ya, kx, vx, w, B, S, Nm, tm_back)
    return xt.reshape(B, S, D)
```

```python
import functools

import jax
import jax.numpy as jnp
from jax import lax
from jax.experimental import pallas as pl
from jax.experimental.pallas import tpu as pltpu

f32 = jnp.float32
bf16 = jnp.bfloat16

EPS = 1e-6
D_MODEL = 1024
M_HEADS = 4
M_WIDTH = 512
M_DV = 128
M_DK = 64
CONV_K = 4
A_HEADS = 4
A_DV = 128
Q_LORA = 256
KV_LORA = 128
NOPE_DIM = 128
ROPE_DIM = 64
A_QK = NOPE_DIM + ROPE_DIM
ROPE_THETA = 10000.0
X_HEADS = 4
X_HD = 128
X_WIDTH = 512
D_FF = 4096
FF_CHUNK = 1024

M_CHUNK = 128
M_DKP = 128
SUB = 512
ATT_T = 256
QK_PAD = 256
ONES_ROWS = 16
N_U, N_O, N_CQ, N_CKV, N_KR, N_END = 0, 512, 1024, 1280, 1408, 1536
T_V, T_GI, T_GF, T_END = 0, 512, 520, 528

VMEM_LIMIT = 56 * 1024 * 1024
LOG2E = 1.4426950408889634


def _dot(a, b):
    return jnp.dot(a, b, preferred_element_type=f32)


def _dot_nt(a, b):
    return lax.dot_general(a, b, (((1,), (1,)), ((), ())), preferred_element_type=f32)


def _rms(x, g, n=None):
    n = x.shape[-1] if n is None else n
    ms = jnp.sum(x * x, axis=-1, keepdims=True) * (1.0 / n)
    return x * lax.rsqrt(ms + EPS) * g


def _sigmoid(x):
    return 1.0 / (1.0 + jnp.exp(-x))


def _log_sigmoid(x):
    return jnp.minimum(x, 0.0) - jnp.log1p(jnp.exp(-jnp.abs(x)))


def _split3(x):
    a = x.astype(bf16)
    r = x - a.astype(f32)
    b = r.astype(bf16)
    c = (r - b.astype(f32)).astype(bf16)
    return a, b, c


def _rope_kernel(pos_ref, inv_ref, sgn_ref, cos_ref, sin_ref):
    ang = pos_ref[...].astype(f32) * inv_ref[...]
    cos_ref[...] = jnp.cos(ang) * jnp.abs(sgn_ref[...])
    sin_ref[...] = jnp.sin(ang) * sgn_ref[...]


def _rope_tables(positions):
    T = positions.size
    tm = SUB
    assert T % tm == 0
    inv = 1.0 / (ROPE_THETA ** (jnp.arange(0, ROPE_DIM, 2, dtype=f32) / ROPE_DIM))
    z = jnp.zeros((ROPE_DIM // 2,), f32)
    o = jnp.ones((ROPE_DIM // 2,), f32)
    inv_p = jnp.concatenate([inv, z, inv, z])[None, :]
    sgn = jnp.concatenate([-o, z, o, z])[None, :]
    return pl.pallas_call(
        _rope_kernel,
        grid=(T // tm,),
        in_specs=[pl.BlockSpec((tm, 1), lambda i: (i, 0)),
                  pl.BlockSpec((1, 128), lambda i: (0, 0)),
                  pl.BlockSpec((1, 128), lambda i: (0, 0))],
        out_specs=[pl.BlockSpec((tm, 128), lambda i: (i, 0)),
                   pl.BlockSpec((tm, 128), lambda i: (i, 0))],
        out_shape=[jax.ShapeDtypeStruct((T, 128), f32)] * 2,
        name="rope_tables",
    )(positions.reshape(T, 1), inv_p, sgn)


def _front_kernel(x_ref, cos_ref, sin_ref, gmix_ref, wnat_ref, wtr_ref, convw_ref, convb_ref,
                  wqt_ref, wk_ref, gbias_ref, mog_ref, cqg_ref, ckvg_ref,
                  wuq_ref, wukvk_ref, wukvvt_ref, gq_ref, gk_ref,
                  ym_ref, q_ref, k_ref, vt_ref,
                  ubuf_ref, st_ref, mst_ref, *, tm):
    L = M_CHUNK

    @pl.when(pl.program_id(1) == 0)
    def _():
        ubuf_ref[0:8, :] = jnp.zeros((8, M_WIDTH), f32)
        st_ref[...] = jnp.zeros(st_ref.shape, f32)
        mst_ref[...] = jnp.zeros(mst_ref.shape, f32)

    srow = lax.broadcasted_iota(jnp.int32, (L, L), 0)
    qcol = lax.broadcasted_iota(jnp.int32, (L, L), 1)
    causal_t = srow <= qcol
    triu = jnp.where(causal_t, 1.0, 0.0).astype(bf16)
    ones_f = jnp.ones((ONES_ROWS, L), f32)
    ones_b = jnp.ones((ONES_ROWS, L), bf16)
    zpad = jnp.zeros((L - 8, L), f32)
    qscale = A_QK ** -0.5 * LOG2E
    gq_n, gq_r = gq_ref[:, 0:128] * qscale, gq_ref[:, 128:256] * qscale
    gk_n, gk_r = gk_ref[:, 0:128], gk_ref[:, 128:256]
    m_run = mst_ref[:, 0:1]

    def project(sub):
        h = _rms(x_ref[sub * SUB:(sub + 1) * SUB, :], gmix_ref[...]).astype(bf16)
        pn = _dot(h, wnat_ref[...])
        pt = _dot_nt(wtr_ref[...], h)
        return pn, pt

    def prepare(sub, pn, pt):
        rows = slice(sub * SUB, (sub + 1) * SUB)

        u = pn[:, N_U:N_U + M_WIDTH]
        ubuf_ref[8:8 + SUB, :] = u
        acc = convb_ref[...] + convw_ref[CONV_K - 1:CONV_K, :] * u
        for j in range(CONV_K - 1):
            off = 8 - (CONV_K - 1) + j
            acc = acc + convw_ref[j:j + 1, :] * ubuf_ref[off:off + SUB, :]
        ubuf_ref[0:8, :] = u[SUB - 8:SUB, :]
        ucb = (acc * _sigmoid(acc)).astype(bf16)

        cos = cos_ref[rows, :]
        sin = sin_ref[rows, :]

        def rope(t):
            return t * cos + pltpu.roll(t, 64, 1) * sin

        cqn = _rms(pn[:, N_CQ:N_CQ + Q_LORA], cqg_ref[...]).astype(bf16)
        qa = _dot(cqn, wuq_ref[...])
        ckvn = _rms(pn[:, N_CKV:N_CKV + KV_LORA], ckvg_ref[...]).astype(bf16)
        kn = _dot(ckvn, wukvk_ref[...])
        vt = _dot_nt(wukvvt_ref[...], ckvn).astype(bf16)
        for c in range(SUB // ATT_T):
            blk = sub * (SUB // ATT_T) + c
            vt_ref[blk * A_HEADS * A_DV:(blk + 1) * A_HEADS * A_DV, :] = vt[:, c * ATT_T:(c + 1) * ATT_T]
        kr = rope(_rms(pn[:, N_KR:N_KR + 128], gk_r, ROPE_DIM)).astype(bf16)
        for hh in range(A_HEADS):
            c0 = hh * QK_PAD
            qn = _rms(qa[:, c0:c0 + 128], gq_n)
            qr = rope(_rms(qa[:, c0 + 128:c0 + 256], gq_r, ROPE_DIM))
            q_ref[rows, c0:c0 + 128] = qn.astype(bf16)
            q_ref[rows, c0 + 128:c0 + 256] = qr.astype(bf16)
            k_ref[rows, c0:c0 + 128] = _rms(kn[:, hh * 128:(hh + 1) * 128], gk_n).astype(bf16)
            k_ref[rows, c0 + 128:c0 + 256] = kr

        qt_all = _dot_nt(wqt_ref[...], ucb).astype(bf16)
        k_all = _dot(ucb, wk_ref[...]).astype(bf16)
        gi = pt[T_GI:T_GI + 8, :] + gbias_ref[0:8, :]
        logf = _log_sigmoid(pt[T_GF:T_GF + 8, :] + gbias_ref[8:16, :])
        return pn, pt, qt_all, k_all, gi, logf

    def recur_states(vals, m_run, states):
        pn, pt, qt_all, k_all, gi, logf = vals
        work = []
        for c in range(SUB // L):
            cols = slice(c * L, (c + 1) * L)
            bsum = _dot(jnp.concatenate(_split3(logf[:, cols]), axis=0), triu)
            b = bsum[0:8] + bsum[8:16] + bsum[16:24]
            ic = gi[:, cols]
            btot = b[:, L - 1:L]
            a = btot - b + ic
            mloc = jnp.max(a, axis=1, keepdims=True)
            wa = jnp.exp(a - mloc)
            m_prev = m_run
            m_run = jnp.maximum(btot + m_prev, mloc)
            s_old = jnp.exp(btot + m_prev - m_run)
            s_loc = jnp.exp(mloc - m_run)
            r_t = jnp.concatenate([ic - b, zpad], axis=0).T
            inter_all = b + m_prev
            for hh in range(M_HEADS):
                k_c = k_all[c * L:(c + 1) * L, hh * M_DKP:(hh + 1) * M_DKP]
                qt_c = qt_all[hh * M_DKP:(hh + 1) * M_DKP, cols]
                vt_c = pt[T_V + hh * M_DV:T_V + (hh + 1) * M_DV, cols]
                st = states[hh]
                sc = _dot(k_c, qt_c)
                vt_w = jnp.concatenate([vt_c, ones_f], axis=0) * wa[hh:hh + 1, :]
                loc = _dot(vt_w.astype(bf16), k_c)
                states[hh] = s_old[hh:hh + 1, :] * st + s_loc[hh:hh + 1, :] * loc
                work.append((c, hh, sc, st, qt_c, vt_c, r_t[:, hh:hh + 1], b[hh:hh + 1, :],
                             inter_all[hh:hh + 1, :]))
        return work, m_run

    def recur_outputs(sub, pn, work):
        for c, hh, sc, st, qt_c, vt_c, r_col, b_row, inter in work:
            hs = slice(hh * M_DV, (hh + 1) * M_DV)
            crow = slice(sub * SUB + c * L, sub * SUB + (c + 1) * L)
            dlog = jnp.where(causal_t, r_col + b_row, -jnp.inf)
            mj = jnp.maximum(inter, jnp.max(dlog, axis=0, keepdims=True))
            qk = (sc * jnp.exp(dlog - mj)).astype(bf16)
            s_int = jnp.exp(inter - mj)
            rhs = jnp.concatenate([(qt_c.astype(f32) * s_int).astype(bf16), qk], axis=0)
            lhs = jnp.concatenate(
                [st.astype(bf16), jnp.concatenate([vt_c.astype(bf16), ones_b], axis=0)], axis=1)
            num = _dot(lhs, rhs)
            den = num[M_DV:M_DV + 1, :]
            hm = num[0:M_DV, :] / jnp.maximum(jnp.abs(den), jnp.exp(-mj))
            hn = hm * lax.rsqrt(jnp.sum(hm * hm, axis=0, keepdims=True) * (1.0 / M_DV) + EPS)
            y = hn.T * mog_ref[:, hs] * _sigmoid(pn[c * L:(c + 1) * L, N_O + hh * M_DV:N_O + (hh + 1) * M_DV])
            ym_ref[crow, hs] = y.astype(bf16)

    states = [st_ref[hh] for hh in range(M_HEADS)]
    nsub = tm // SUB
    pn, pt = project(0)
    vals = prepare(0, pn, pt)
    for sub in range(nsub):
        cur = vals
        if sub + 1 < nsub:
            pn, pt = project(sub + 1)
            vals = prepare(sub + 1, pn, pt)
        work, m_run = recur_states(cur, m_run, states)
        recur_outputs(sub, cur[0], work)
    for hh in range(M_HEADS):
        st_ref[hh] = states[hh]
    mst_ref[...] = jnp.broadcast_to(m_run, (8, 128))


def _front(x, cos, sin, w, B, S, tm):
    T = B * S
    nt = S // tm
    vt_rows = (tm // ATT_T) * A_HEADS * A_DV
    row = lambda b, i: (b * nt + i, 0)
    c2 = lambda b, i: (0, 0)

    def full(a):
        return pl.BlockSpec(a.shape, c2)

    weights = [w["gmix"], w["wnat"], w["wtr"], w["convw"], w["convb"], w["wqt"], w["wk"], w["gbias"],
               w["mog"], w["cqg"], w["ckvg"], w["wuq"], w["wukvk"], w["wukvvt"], w["gq"], w["gk"]]
    return pl.pallas_call(
        functools.partial(_front_kernel, tm=tm),
        grid=(B, nt),
        in_specs=[pl.BlockSpec((tm, D_MODEL), row),
                  pl.BlockSpec((tm, 128), row),
                  pl.BlockSpec((tm, 128), row)] + [full(a) for a in weights],
        out_specs=[pl.BlockSpec((tm, M_WIDTH), row),
                   pl.BlockSpec((tm, A_HEADS * QK_PAD), row),
                   pl.BlockSpec((tm, A_HEADS * QK_PAD), row),
                   pl.BlockSpec((vt_rows, ATT_T), row)],
        out_shape=[jax.ShapeDtypeStruct((T, M_WIDTH), bf16),
                   jax.ShapeDtypeStruct((T, A_HEADS * QK_PAD), bf16),
                   jax.ShapeDtypeStruct((T, A_HEADS * QK_PAD), bf16),
                   jax.ShapeDtypeStruct((B * nt * vt_rows, ATT_T), bf16)],
        scratch_shapes=[pltpu.VMEM((SUB + 8, M_WIDTH), f32),
                        pltpu.VMEM((M_HEADS, M_DV + ONES_ROWS, M_DKP), f32),
                        pltpu.VMEM((8, 128), f32)],
        compiler_params=pltpu.CompilerParams(
            dimension_semantics=("arbitrary", "arbitrary"), vmem_limit_bytes=VMEM_LIMIT),
        name="front",
    )(x, cos, sin, *weights)


def _attn_kernel(q_ref, k_ref, vt_ref, g_ref, o_ref, acc_ref, *, tq):
    qi = pl.program_id(1)
    ones = jnp.ones((ONES_ROWS, tq), bf16)
    acc_ref[...] = jnp.zeros(acc_ref.shape, f32)

    def block(j, ms, masked):
        r0 = pl.multiple_of(j * tq, tq)
        v0 = j * (A_HEADS * A_DV)
        out = []
        for hh in range(A_HEADS):
            c0 = hh * QK_PAD
            s = _dot_nt(k_ref[pl.ds(r0, tq), c0:c0 + QK_PAD], q_ref[:, c0:c0 + QK_PAD])
            if masked:
                row = lax.broadcasted_iota(jnp.int32, (tq, tq), 0)
                col = lax.broadcasted_iota(jnp.int32, (tq, tq), 1)
                s = jnp.where(row <= col, s, -jnp.inf)
            m_new = jnp.maximum(ms[hh], jnp.max(s, axis=0, keepdims=True))
            alpha = jnp.exp2(ms[hh] - m_new)
            p = jnp.exp2(s - m_new).astype(bf16)
            vte = jnp.concatenate(
                [vt_ref[pl.ds(pl.multiple_of(v0 + hh * A_DV, A_DV), A_DV), :], ones], axis=0)
            acc_ref[hh] = alpha * acc_ref[hh] + _dot(vte, p)
            out.append(m_new)
        return tuple(out)

    ms = tuple(jnp.full((1, tq), -jnp.inf, f32) for _ in range(A_HEADS))
    ms = lax.fori_loop(0, qi, lambda j, c: block(j, c, False), ms)
    block(qi, ms, True)
    for hh in range(A_HEADS):
        a = acc_ref[hh]
        o = a[0:A_DV, :] / a[A_DV:A_DV + 1, :]
        ms2 = jnp.sum(o * o, axis=0, keepdims=True) * (1.0 / A_DV)
        y = o * lax.rsqrt(ms2 + EPS) * g_ref[hh * A_DV:(hh + 1) * A_DV, :]
        o_ref[:, hh * A_DV:(hh + 1) * A_DV] = y.T.astype(bf16)


def _attention(q, k, vt, g, B, S, tq):
    T = B * S
    nq = S // tq
    return pl.pallas_call(
        functools.partial(_attn_kernel, tq=tq),
        grid=(B, nq),
        in_specs=[pl.BlockSpec((tq, A_HEADS * QK_PAD), lambda b, i: (b * nq + i, 0)),
                  pl.BlockSpec((S, A_HEADS * QK_PAD), lambda b, i: (b, 0)),
                  pl.BlockSpec((nq * A_HEADS * A_DV, tq), lambda b, i: (b, 0)),
                  pl.BlockSpec((A_HEADS * A_DV, 1), lambda b, i: (0, 0))],
        out_specs=pl.BlockSpec((tq, A_HEADS * A_DV), lambda b, i: (b * nq + i, 0)),
        out_shape=jax.ShapeDtypeStruct((T, A_HEADS * A_DV), bf16),
        scratch_shapes=[pltpu.VMEM((A_HEADS, A_DV + ONES_ROWS, tq), f32)],
        compiler_params=pltpu.CompilerParams(
            dimension_semantics=("arbitrary", "arbitrary"), vmem_limit_bytes=VMEM_LIMIT),
        name="mla_attention",
    )(q, k, vt, g)


def _memkv_kernel(mem_ref, g_ref, w_ref, kg_ref, k_ref, v_ref):
    mn = _rms(mem_ref[...], g_ref[...]).astype(bf16)
    kv = _dot(mn, w_ref[...])
    for hh in range(X_HEADS):
        sl = slice(hh * X_HD, (hh + 1) * X_HD)
        k_ref[:, sl] = _rms(kv[:, sl], kg_ref[...]).astype(bf16)
    v_ref[...] = kv[:, X_WIDTH:].astype(bf16)


def _memkv(mem2d, g, wkv, kg, B, Nm):
    return pl.pallas_call(
        _memkv_kernel,
        grid=(B,),
        in_specs=[pl.BlockSpec((Nm, D_MODEL), lambda b: (b, 0)),
                  pl.BlockSpec((1, D_MODEL), lambda b: (0, 0)),
                  pl.BlockSpec((D_MODEL, 2 * X_WIDTH), lambda b: (0, 0)),
                  pl.BlockSpec((1, X_HD), lambda b: (0, 0))],
        out_specs=[pl.BlockSpec((Nm, X_WIDTH), lambda b: (b, 0)),
                   pl.BlockSpec((Nm, X_WIDTH), lambda b: (b, 0))],
        out_shape=[jax.ShapeDtypeStruct((B * Nm, X_WIDTH), bf16)] * 2,
        compiler_params=pltpu.CompilerParams(
            dimension_semantics=("arbitrary",), vmem_limit_bytes=VMEM_LIMIT),
        name="mem_kv",
    )(mem2d, g, wkv, kg)


def _back_kernel(x_ref, ym_ref, ya_ref, wout_ref, gx_ref, wqx_ref, xqg_ref, kx_ref, vx_ref,
                 wox_ref, gf_ref, w1_ref, w2_ref, out_ref):
    x1 = (x_ref[...] + _dot(ym_ref[...], wout_ref[0:M_WIDTH, :])
          + _dot(ya_ref[...], wout_ref[M_WIDTH:2 * M_WIDTH, :]))

    hn = _rms(x1, gx_ref[...]).astype(bf16)
    qx = _dot(hn, wqx_ref[...])
    heads = []
    for hh in range(X_HEADS):
        sl = slice(hh * X_HD, (hh + 1) * X_HD)
        qh = (_rms(qx[:, sl], xqg_ref[...]) * (X_HD ** -0.5)).astype(bf16)
        s = _dot_nt(qh, kx_ref[:, sl])
        p = jnp.exp(s - jnp.max(s, axis=-1, keepdims=True))
        o = _dot(p.astype(bf16), vx_ref[:, sl]) / jnp.sum(p, axis=-1, keepdims=True)
        heads.append(o.astype(bf16))
    x2 = x1 + _dot(jnp.concatenate(heads, axis=-1), wox_ref[...])

    hf = _rms(x2, gf_ref[...]).astype(bf16)
    acc = x2
    for c in range(D_FF // FF_CHUNK):
        sl = slice(c * FF_CHUNK, (c + 1) * FF_CHUNK)
        a = jnp.maximum(_dot(hf, w1_ref[:, sl]), 0.0)
        acc = acc + _dot((a * a).astype(bf16), w2_ref[sl, :])
    out_ref[...] = acc


def _back(x, ym, ya, kx, vx, w, B, S, Nm, tm):
    T = B * S
    row = lambda i: (i, 0)
    c2 = lambda i: (0, 0)
    per_b = lambda i: ((i * tm) // S, 0)

    def const(a):
        return pl.BlockSpec(a.shape, c2, pipeline_mode=pl.Buffered(1))

    return pl.pallas_call(
        _back_kernel,
        grid=(T // tm,),
        in_specs=[pl.BlockSpec((tm, D_MODEL), row),
                  pl.BlockSpec((tm, M_WIDTH), row),
                  pl.BlockSpec((tm, M_WIDTH), row),
                  const(w["wout"]), const(w["gx"]), const(w["wqx"]), const(w["xqg"]),
                  pl.BlockSpec((Nm, X_WIDTH), per_b),
                  pl.BlockSpec((Nm, X_WIDTH), per_b),
                  const(w["wox"]), const(w["gf"]), const(w["w1"]), const(w["w2"])],
        out_specs=pl.BlockSpec((tm, D_MODEL), row),
        out_shape=jax.ShapeDtypeStruct((T, D_MODEL), f32),
        compiler_params=pltpu.CompilerParams(
            dimension_semantics=("arbitrary",), vmem_limit_bytes=VMEM_LIMIT),
        name="back",
    )(x, ym, ya, w["wout"], w["gx"], w["wqx"], w["xqg"], kx, vx, w["wox"], w["gf"], w["w1"], w["w2"])


def _rope_pad(a):
    z = jnp.zeros(a.shape[:-1] + (ROPE_DIM // 2,), a.dtype)
    return jnp.concatenate([a[..., :32], z, a[..., 32:], z], axis=-1)


def _block_diag_heads(w):
    H = w.shape[0]
    wp = jnp.pad(w, ((0, 0), (0, 0), (0, M_DKP - M_DK)))
    out = jnp.zeros((H, M_DV, H, M_DKP), w.dtype)
    for hh in range(H):
        out = out.at[hh, :, hh, :].set(wp[hh])
    return out.reshape(H * M_DV, H * M_DKP)


def _layer_weights(l, norm_mix_g, w_in, conv_w, conv_b, wq_m, wk_m, b_igate, b_fgate, m_out_g,
                   cq_norm_g, ckv_norm_g, w_uq, w_ukv, qk_norm_q, qk_norm_k, a_out_g, w_out,
                   norm_x_g, norm_mem_g, wq_x, wkv_x, xq_norm_g, xk_norm_g, wo_x,
                   norm_ffn_g, w_ff1, w_ff2):
    wi = w_in[l]
    o_v, o_o, o_g = M_WIDTH, 2 * M_WIDTH, 3 * M_WIDTH
    o_f = o_g + M_HEADS
    o_cq = o_f + M_HEADS
    o_ckv = o_cq + Q_LORA
    o_kr = o_ckv + KV_LORA
    wnat = jnp.concatenate([wi[:, :o_v], wi[:, o_o:o_g], wi[:, o_cq:o_kr], _rope_pad(wi[:, o_kr:])], axis=1)
    zg = jnp.zeros((8 - M_HEADS, D_MODEL), wi.dtype)
    wtr = jnp.concatenate([wi[:, o_v:o_o].T, wi[:, o_g:o_f].T, zg, wi[:, o_f:o_cq].T, zg], axis=0)
    z4 = jnp.zeros((8 - M_HEADS,), f32)
    gbias = jnp.concatenate([b_igate[l], z4, b_fgate[l], z4])[:, None]
    wuq = w_uq[l].reshape(Q_LORA, A_HEADS, A_QK)
    wuq = jnp.concatenate([wuq[..., :NOPE_DIM], _rope_pad(wuq[..., NOPE_DIM:])], axis=-1)
    wukv = w_ukv[l].reshape(KV_LORA, A_HEADS, NOPE_DIM + A_DV)
    return {
        "gmix": norm_mix_g[l][None, :],
        "wnat": wnat.astype(bf16),
        "wtr": wtr.astype(bf16),
        "convw": conv_w[l],
        "convb": conv_b[l][None, :],
        "wqt": (_block_diag_heads(wq_m[l]) * (M_DK ** -0.5)).T.astype(bf16),
        "wk": _block_diag_heads(wk_m[l]).astype(bf16),
        "gbias": gbias,
        "mog": m_out_g[l].reshape(1, M_WIDTH),
        "cqg": cq_norm_g[l][None, :],
        "ckvg": ckv_norm_g[l][None, :],
        "wuq": wuq.reshape(Q_LORA, A_HEADS * QK_PAD).astype(bf16),
        "wukvk": wukv[..., :NOPE_DIM].reshape(KV_LORA, A_HEADS * NOPE_DIM).astype(bf16),
        "wukvvt": wukv[..., NOPE_DIM:].reshape(KV_LORA, A_HEADS * A_DV).T.astype(bf16),
        "gq": jnp.concatenate([qk_norm_q[l][:NOPE_DIM], _rope_pad(qk_norm_q[l][NOPE_DIM:])])[None, :],
        "gk": jnp.concatenate([qk_norm_k[l][:NOPE_DIM], _rope_pad(qk_norm_k[l][NOPE_DIM:])])[None, :],
        "aog": a_out_g[l].reshape(A_HEADS * A_DV, 1),
        "wout": w_out[l].astype(bf16),
        "gx": norm_x_g[l][None, :],
        "gmem": norm_mem_g[l][None, :],
        "wqx": wq_x[l].astype(bf16),
        "wkvx": wkv_x[l].astype(bf16),
        "xqg": xq_norm_g[l][None, :],
        "xkg": xk_norm_g[l][None, :],
        "wox": wo_x[l].astype(bf16),
        "gf": norm_ffn_g[l][None, :],
        "w1": w_ff1[l].astype(bf16),
        "w2": w_ff2[l].astype(bf16),
    }


def kernel(x, mem, positions, norm_mix_g, w_in, conv_w, conv_b, wq_m, wk_m, b_igate, b_fgate, m_out_g, cq_norm_g, ckv_norm_g, w_uq, w_ukv, qk_norm_q, qk_norm_k, a_out_g, w_out, norm_x_g, norm_mem_g, wq_x, wkv_x, xq_norm_g, xk_norm_g, wo_x, norm_ffn_g, w_ff1, w_ff2):
    B, S, D = x.shape
    Nm = mem.shape[1]
    depth = w_in.shape[0]
    assert D == D_MODEL and S % (2 * SUB) == 0
    tm_front = 2 * SUB
    tm_back = SUB
    params = (norm_mix_g, w_in, conv_w, conv_b, wq_m, wk_m, b_igate, b_fgate, m_out_g,
              cq_norm_g, ckv_norm_g, w_uq, w_ukv, qk_norm_q, qk_norm_k, a_out_g, w_out,
              norm_x_g, norm_mem_g, wq_x, wkv_x, xq_norm_g, xk_norm_g, wo_x,
              norm_ffn_g, w_ff1, w_ff2)

    cos, sin = _rope_tables(positions)
    xt = x.reshape(B * S, D)
    mem2d = mem.reshape(B * Nm, D)
    for l in range(depth):
        w = _layer_weights(l, *params)
        ym, q, k, vt = _front(xt, cos, sin, w, B, S, tm_front)
        ya = _attention(q, k, vt, w["aog"], B, S, ATT_T)
        kx, vx = _memkv(mem2d, w["gmem"], w["wkvx"], w["xkg"], B, Nm)
        xt = _back(xt, ym, ya, kx, vx, w, B, S, Nm, tm_back)
    return xt.reshape(B, S, D)
```

```python
import functools

import jax
import jax.numpy as jnp
from jax import lax
from jax.experimental import pallas as pl
from jax.experimental.pallas import tpu as pltpu

f32 = jnp.float32
bf16 = jnp.bfloat16

EPS = 1e-6
D_MODEL = 1024
M_HEADS = 4
M_WIDTH = 512
M_DV = 128
M_DK = 64
CONV_K = 4
A_HEADS = 4
A_DV = 128
Q_LORA = 256
KV_LORA = 128
NOPE_DIM = 128
ROPE_DIM = 64
A_QK = NOPE_DIM + ROPE_DIM
ROPE_THETA = 10000.0
X_HEADS = 4
X_HD = 128
X_WIDTH = 512
D_FF = 4096
FF_CHUNK = 1024

M_CHUNK = 128
M_DKP = 128
SUB = 512
ATT_T = 256
QK_PAD = 256
ONES_ROWS = 16
N_U, N_O, N_CQ, N_CKV, N_KR, N_END = 0, 512, 1024, 1280, 1408, 1536
T_V, T_GI, T_GF, T_END = 0, 512, 520, 528

VMEM_LIMIT = 56 * 1024 * 1024
LOG2E = 1.4426950408889634


def _dot(a, b):
    return jnp.dot(a, b, preferred_element_type=f32)


def _dot_nt(a, b):
    return lax.dot_general(a, b, (((1,), (1,)), ((), ())), preferred_element_type=f32)


def _rms(x, g, n=None):
    n = x.shape[-1] if n is None else n
    ms = jnp.sum(x * x, axis=-1, keepdims=True) * (1.0 / n)
    return x * lax.rsqrt(ms + EPS) * g


def _sigmoid(x):
    return 1.0 / (1.0 + jnp.exp(-x))


def _log_sigmoid(x):
    return jnp.minimum(x, 0.0) - jnp.log1p(jnp.exp(-jnp.abs(x)))


def _split3(x):
    a = x.astype(bf16)
    r = x - a.astype(f32)
    b = r.astype(bf16)
    c = (r - b.astype(f32)).astype(bf16)
    return a, b, c


def _rope_kernel(pos_ref, inv_ref, sgn_ref, cos_ref, sin_ref):
    ang = pos_ref[...].astype(f32) * inv_ref[...]
    cos_ref[...] = jnp.cos(ang) * jnp.abs(sgn_ref[...])
    sin_ref[...] = jnp.sin(ang) * sgn_ref[...]


def _rope_tables(positions):
    T = positions.size
    tm = 2 * SUB
    assert T % tm == 0
    inv = 1.0 / (ROPE_THETA ** (jnp.arange(0, ROPE_DIM, 2, dtype=f32) / ROPE_DIM))
    z = jnp.zeros((ROPE_DIM // 2,), f32)
    o = jnp.ones((ROPE_DIM // 2,), f32)
    inv_p = jnp.concatenate([inv, z, inv, z])[None, :]
    sgn = jnp.concatenate([-o, z, o, z])[None, :]
    return pl.pallas_call(
        _rope_kernel,
        grid=(T // tm,),
        in_specs=[pl.BlockSpec((tm, 1), lambda i: (i, 0)),
                  pl.BlockSpec((1, 128), lambda i: (0, 0)),
                  pl.BlockSpec((1, 128), lambda i: (0, 0))],
        out_specs=[pl.BlockSpec((tm, 128), lambda i: (i, 0)),
                   pl.BlockSpec((tm, 128), lambda i: (i, 0))],
        out_shape=[jax.ShapeDtypeStruct((T, 128), f32)] * 2,
        name="rope_tables",
    )(positions.reshape(T, 1), inv_p, sgn)


def _front_kernel(x_ref, cos_ref, sin_ref, gmix_ref, wnat_ref, wtr_ref, convw_ref, convb_ref,
                  wqt_ref, wk_ref, gbias_ref, mog_ref, cqg_ref, ckvg_ref,
                  wuq_ref, wukvk_ref, wukvvt_ref, gq_ref, gk_ref,
                  ym_ref, q_ref, k_ref, vt_ref,
                  ubuf_ref, st_ref, mst_ref, lhs_ref, rhs_ref, num_ref, *, tm):
    L = M_CHUNK

    @pl.when(pl.program_id(1) == 0)
    def _():
        ubuf_ref[0:8, :] = jnp.zeros((8, M_WIDTH), f32)
        st_ref[...] = jnp.zeros(st_ref.shape, f32)
        mst_ref[...] = jnp.zeros(mst_ref.shape, f32)

    srow = lax.broadcasted_iota(jnp.int32, (L, L), 0)
    qcol = lax.broadcasted_iota(jnp.int32, (L, L), 1)
    causal_t = srow <= qcol
    triu = jnp.where(causal_t, 1.0, 0.0).astype(bf16)
    ones_f = jnp.ones((ONES_ROWS, L), f32)
    ones_b = jnp.ones((ONES_ROWS, L), bf16)
    zpad = jnp.zeros((L - 8, L), f32)
    qscale = A_QK ** -0.5 * LOG2E
    gq_n, gq_r = gq_ref[:, 0:128] * qscale, gq_ref[:, 128:256] * qscale
    gk_n, gk_r = gk_ref[:, 0:128], gk_ref[:, 128:256]
    m_run = mst_ref[:, 0:1]

    def project(sub):
        h = _rms(x_ref[sub * SUB:(sub + 1) * SUB, :], gmix_ref[...]).astype(bf16)
        pn = _dot(h, wnat_ref[...])
        pt = _dot_nt(wtr_ref[...], h)
        return pn, pt

    def prepare(sub, pn, pt):
        rows = slice(sub * SUB, (sub + 1) * SUB)

        u = pn[:, N_U:N_U + M_WIDTH]
        ubuf_ref[8:8 + SUB, :] = u
        acc = convb_ref[...] + convw_ref[CONV_K - 1:CONV_K, :] * u
        for j in range(CONV_K - 1):
            off = 8 - (CONV_K - 1) + j
            acc = acc + convw_ref[j:j + 1, :] * ubuf_ref[off:off + SUB, :]
        ubuf_ref[0:8, :] = u[SUB - 8:SUB, :]
        ucb = (acc * _sigmoid(acc)).astype(bf16)

        cos = cos_ref[rows, :]
        sin = sin_ref[rows, :]

        def rope(t):
            return t * cos + pltpu.roll(t, 64, 1) * sin

        cqn = _rms(pn[:, N_CQ:N_CQ + Q_LORA], cqg_ref[...]).astype(bf16)
        qa = _dot(cqn, wuq_ref[...])
        ckvn = _rms(pn[:, N_CKV:N_CKV + KV_LORA], ckvg_ref[...]).astype(bf16)
        kn = _dot(ckvn, wukvk_ref[...])
        vt = _dot_nt(wukvvt_ref[...], ckvn).astype(bf16)
        for c in range(SUB // ATT_T):
            blk = sub * (SUB // ATT_T) + c
            vt_ref[blk * A_HEADS * A_DV:(blk + 1) * A_HEADS * A_DV, :] = vt[:, c * ATT_T:(c + 1) * ATT_T]
        kr = rope(_rms(pn[:, N_KR:N_KR + 128], gk_r, ROPE_DIM)).astype(bf16)
        for hh in range(A_HEADS):
            c0 = hh * QK_PAD
            qn = _rms(qa[:, c0:c0 + 128], gq_n)
            qr = rope(_rms(qa[:, c0 + 128:c0 + 256], gq_r, ROPE_DIM))
            q_ref[rows, c0:c0 + 128] = qn.astype(bf16)
            q_ref[rows, c0 + 128:c0 + 256] = qr.astype(bf16)
            k_ref[rows, c0:c0 + 128] = _rms(kn[:, hh * 128:(hh + 1) * 128], gk_n).astype(bf16)
            k_ref[rows, c0 + 128:c0 + 256] = kr

        qt_all = _dot_nt(wqt_ref[...], ucb).astype(bf16)
        k_all = _dot(ucb, wk_ref[...]).astype(bf16)
        gi = pt[T_GI:T_GI + 8, :] + gbias_ref[0:8, :]
        logf = _log_sigmoid(pt[T_GF:T_GF + 8, :] + gbias_ref[8:16, :])
        return pn, pt, qt_all, k_all, gi, logf

    def recur_states(vals, m_run, states):
        pn, pt, qt_all, k_all, gi, logf = vals
        work = []
        for c in range(SUB // L):
            cols = slice(c * L, (c + 1) * L)
            bsum = _dot(jnp.concatenate(_split3(logf[:, cols]), axis=0), triu)
            b = bsum[0:8] + bsum[8:16] + bsum[16:24]
            ic = gi[:, cols]
            btot = b[:, L - 1:L]
            a = btot - b + ic
            mloc = jnp.max(a, axis=1, keepdims=True)
            wa = jnp.exp(a - mloc)
            m_prev = m_run
            m_run = jnp.maximum(btot + m_prev, mloc)
            s_old = jnp.exp(btot + m_prev - m_run)
            s_loc = jnp.exp(mloc - m_run)
            r_t = jnp.concatenate([ic - b, zpad], axis=0).T
            inter_all = b + m_prev
            for hh in range(M_HEADS):
                k_c = k_all[c * L:(c + 1) * L, hh * M_DKP:(hh + 1) * M_DKP]
                qt_c = qt_all[hh * M_DKP:(hh + 1) * M_DKP, cols]
                vt_c = pt[T_V + hh * M_DV:T_V + (hh + 1) * M_DV, cols]
                st = states[hh]
                sc = _dot(k_c, qt_c)
                vt_w = jnp.concatenate([vt_c, ones_f], axis=0) * wa[hh:hh + 1, :]
                loc = _dot(vt_w.astype(bf16), k_c)
                states[hh] = s_old[hh:hh + 1, :] * st + s_loc[hh:hh + 1, :] * loc
                work.append((c, hh, sc, st, qt_c, vt_c, r_t[:, hh:hh + 1], b[hh:hh + 1, :],
                             inter_all[hh:hh + 1, :]))
        return work, m_run

    def recur_outputs(sub, pn, work):
        mjs = []
        for i, (c, hh, sc, st, qt_c, vt_c, r_col, b_row, inter) in enumerate(work):
            dlog = jnp.where(causal_t, r_col + b_row, -jnp.inf)
            mj = jnp.maximum(inter, jnp.max(dlog, axis=0, keepdims=True))
            qk = (sc * jnp.exp(dlog - mj)).astype(bf16)
            s_int = jnp.exp(inter - mj)
            rhs_ref[i] = jnp.concatenate([(qt_c.astype(f32) * s_int).astype(bf16), qk], axis=0)
            lhs_ref[i] = jnp.concatenate(
                [st.astype(bf16), jnp.concatenate([vt_c.astype(bf16), ones_b], axis=0)], axis=1)
            mjs.append(mj)
        for i in range(len(work)):
            num_ref[i] = _dot(lhs_ref[i], rhs_ref[i])
        for i, (c, hh, sc, st, qt_c, vt_c, r_col, b_row, inter) in enumerate(work):
            hs = slice(hh * M_DV, (hh + 1) * M_DV)
            crow = slice(sub * SUB + c * L, sub * SUB + (c + 1) * L)
            num = num_ref[i]
            den = num[M_DV:M_DV + 1, :]
            hm = num[0:M_DV, :] / jnp.maximum(jnp.abs(den), jnp.exp(-mjs[i]))
            hn = hm * lax.rsqrt(jnp.sum(hm * hm, axis=0, keepdims=True) * (1.0 / M_DV) + EPS)
            y = hn.T * mog_ref[:, hs] * _sigmoid(pn[c * L:(c + 1) * L, N_O + hh * M_DV:N_O + (hh + 1) * M_DV])
            ym_ref[crow, hs] = y.astype(bf16)

    states = [st_ref[hh] for hh in range(M_HEADS)]
    nsub = tm // SUB
    pn, pt = project(0)
    vals = prepare(0, pn, pt)
    for sub in range(nsub):
        cur = vals
        if sub + 1 < nsub:
            pn, pt = project(sub + 1)
            vals = prepare(sub + 1, pn, pt)
        work, m_run = recur_states(cur, m_run, states)
        recur_outputs(sub, cur[0], work)
    for hh in range(M_HEADS):
        st_ref[hh] = states[hh]
    mst_ref[...] = jnp.broadcast_to(m_run, (8, 128))


def _front(x, cos, sin, w, B, S, tm):
    T = B * S
    nt = S // tm
    vt_rows = (tm // ATT_T) * A_HEADS * A_DV
    nch = (SUB // M_CHUNK) * M_HEADS
    row = lambda b, i: (b * nt + i, 0)
    c2 = lambda b, i: (0, 0)

    def full(a):
        return pl.BlockSpec(a.shape, c2)

    weights = [w["gmix"], w["wnat"], w["wtr"], w["convw"], w["convb"], w["wqt"], w["wk"], w["gbias"],
               w["mog"], w["cqg"], w["ckvg"], w["wuq"], w["wukvk"], w["wukvvt"], w["gq"], w["gk"]]
    return pl.pallas_call(
        functools.partial(_front_kernel, tm=tm),
        grid=(B, nt),
        in_specs=[pl.BlockSpec((tm, D_MODEL), row),
                  pl.BlockSpec((tm, 128), row),
                  pl.BlockSpec((tm, 128), row)] + [full(a) for a in weights],
        out_specs=[pl.BlockSpec((tm, M_WIDTH), row),
                   pl.BlockSpec((tm, A_HEADS * QK_PAD), row),
                   pl.BlockSpec((tm, A_HEADS * QK_PAD), row),
                   pl.BlockSpec((vt_rows, ATT_T), row)],
        out_shape=[jax.ShapeDtypeStruct((T, M_WIDTH), bf16),
                   jax.ShapeDtypeStruct((T, A_HEADS * QK_PAD), bf16),
                   jax.ShapeDtypeStruct((T, A_HEADS * QK_PAD), bf16),
                   jax.ShapeDtypeStruct((B * nt * vt_rows, ATT_T), bf16)],
        scratch_shapes=[pltpu.VMEM((SUB + 8, M_WIDTH), f32),
                        pltpu.VMEM((M_HEADS, M_DV + ONES_ROWS, M_DKP), f32),
                        pltpu.VMEM((8, 128), f32),
                        pltpu.VMEM((nch, M_DV + ONES_ROWS, M_DKP + M_CHUNK), bf16),
                        pltpu.VMEM((nch, M_DKP + M_CHUNK, M_CHUNK), bf16),
                        pltpu.VMEM((nch, M_DV + ONES_ROWS, M_CHUNK), f32)],
        compiler_params=pltpu.CompilerParams(
            dimension_semantics=("arbitrary", "arbitrary"), vmem_limit_bytes=VMEM_LIMIT),
        name="front",
    )(x, cos, sin, *weights)


def _attn_kernel(q_ref, k_ref, vt_ref, g_ref, o_ref, acc_ref, *, tq):
    qi = pl.program_id(1)
    ones = jnp.ones((ONES_ROWS, tq), bf16)
    acc_ref[...] = jnp.zeros(acc_ref.shape, f32)

    def block(j, ms, masked):
        r0 = pl.multiple_of(j * tq, tq)
        v0 = j * (A_HEADS * A_DV)
        out = []
        for hh in range(A_HEADS):
            c0 = hh * QK_PAD
            s = _dot_nt(k_ref[pl.ds(r0, tq), c0:c0 + QK_PAD], q_ref[:, c0:c0 + QK_PAD])
            if masked:
                row = lax.broadcasted_iota(jnp.int32, (tq, tq), 0)
                col = lax.broadcasted_iota(jnp.int32, (tq, tq), 1)
                s = jnp.where(row <= col, s, -jnp.inf)
            m_new = jnp.maximum(ms[hh], jnp.max(s, axis=0, keepdims=True))
            alpha = jnp.exp2(ms[hh] - m_new)
            p = jnp.exp2(s - m_new).astype(bf16)
            vte = jnp.concatenate(
                [vt_ref[pl.ds(pl.multiple_of(v0 + hh * A_DV, A_DV), A_DV), :], ones], axis=0)
            acc_ref[hh] = alpha * acc_ref[hh] + _dot(vte, p)
            out.append(m_new)
        return tuple(out)

    ms = tuple(jnp.full((1, tq), -jnp.inf, f32) for _ in range(A_HEADS))
    ms = lax.fori_loop(0, qi, lambda j, c: block(j, c, False), ms)
    block(qi, ms, True)
    for hh in range(A_HEADS):
        a = acc_ref[hh]
        o = a[0:A_DV, :] / a[A_DV:A_DV + 1, :]
        ms2 = jnp.sum(o * o, axis=0, keepdims=True) * (1.0 / A_DV)
        y = o * lax.rsqrt(ms2 + EPS) * g_ref[hh * A_DV:(hh + 1) * A_DV, :]
        o_ref[:, hh * A_DV:(hh + 1) * A_DV] = y.T.astype(bf16)


def _attention(q, k, vt, g, B, S, tq):
    T = B * S
    nq = S // tq
    return pl.pallas_call(
        functools.partial(_attn_kernel, tq=tq),
        grid=(B, nq),
        in_specs=[pl.BlockSpec((tq, A_HEADS * QK_PAD), lambda b, i: (b * nq + i, 0)),
                  pl.BlockSpec((S, A_HEADS * QK_PAD), lambda b, i: (b, 0)),
                  pl.BlockSpec((nq * A_HEADS * A_DV, tq), lambda b, i: (b, 0)),
                  pl.BlockSpec((A_HEADS * A_DV, 1), lambda b, i: (0, 0))],
        out_specs=pl.BlockSpec((tq, A_HEADS * A_DV), lambda b, i: (b * nq + i, 0)),
        out_shape=jax.ShapeDtypeStruct((T, A_HEADS * A_DV), bf16),
        scratch_shapes=[pltpu.VMEM((A_HEADS, A_DV + ONES_ROWS, tq), f32)],
        compiler_params=pltpu.CompilerParams(
            dimension_semantics=("arbitrary", "arbitrary"), vmem_limit_bytes=VMEM_LIMIT),
        name="mla_attention",
    )(q, k, vt, g)


def _memkv_kernel(mem_ref, g_ref, w_ref, kg_ref, k_ref, v_ref):
    mn = _rms(mem_ref[...], g_ref[...]).astype(bf16)
    kv = _dot(mn, w_ref[...])
    for hh in range(X_HEADS):
        sl = slice(hh * X_HD, (hh + 1) * X_HD)
        k_ref[:, sl] = _rms(kv[:, sl], kg_ref[...]).astype(bf16)
    v_ref[...] = kv[:, X_WIDTH:].astype(bf16)


def _memkv(mem2d, g, wkv, kg, B, Nm):
    return pl.pallas_call(
        _memkv_kernel,
        grid=(B,),
        in_specs=[pl.BlockSpec((Nm, D_MODEL), lambda b: (b, 0)),
                  pl.BlockSpec((1, D_MODEL), lambda b: (0, 0)),
                  pl.BlockSpec((D_MODEL, 2 * X_WIDTH), lambda b: (0, 0)),
                  pl.BlockSpec((1, X_HD), lambda b: (0, 0))],
        out_specs=[pl.BlockSpec((Nm, X_WIDTH), lambda b: (b, 0)),
                   pl.BlockSpec((Nm, X_WIDTH), lambda b: (b, 0))],
        out_shape=[jax.ShapeDtypeStruct((B * Nm, X_WIDTH), bf16)] * 2,
        compiler_params=pltpu.CompilerParams(
            dimension_semantics=("arbitrary",), vmem_limit_bytes=VMEM_LIMIT),
        name="mem_kv",
    )(mem2d, g, wkv, kg)


def _back_kernel(x_ref, ym_ref, ya_ref, wout_ref, gx_ref, wqx_ref, xqg_ref, kx_ref, vx_ref,
                 wox_ref, gf_ref, w1_ref, w2_ref, out_ref):
    x1 = (x_ref[...] + _dot(ym_ref[...], wout_ref[0:M_WIDTH, :])
          + _dot(ya_ref[...], wout_ref[M_WIDTH:2 * M_WIDTH, :]))

    hn = _rms(x1, gx_ref[...]).astype(bf16)
    qx = _dot(hn, wqx_ref[...])
    heads = []
    for hh in range(X_HEADS):
        sl = slice(hh * X_HD, (hh + 1) * X_HD)
        qh = (_rms(qx[:, sl], xqg_ref[...]) * (X_HD ** -0.5)).astype(bf16)
        s = _dot_nt(qh, kx_ref[:, sl])
        p = jnp.exp(s - jnp.max(s, axis=-1, keepdims=True))
        o = _dot(p.astype(bf16), vx_ref[:, sl]) / jnp.sum(p, axis=-1, keepdims=True)
        heads.append(o.astype(bf16))
    x2 = x1 + _dot(jnp.concatenate(heads, axis=-1), wox_ref[...])

    hf = _rms(x2, gf_ref[...]).astype(bf16)
    acc = x2
    for c in range(D_FF // FF_CHUNK):
        sl = slice(c * FF_CHUNK, (c + 1) * FF_CHUNK)
        a = jnp.maximum(_dot(hf, w1_ref[:, sl]), 0.0)
        acc = acc + _dot((a * a).astype(bf16), w2_ref[sl, :])
    out_ref[...] = acc


def _back(x, ym, ya, kx, vx, w, B, S, Nm, tm):
    T = B * S
    row = lambda i: (i, 0)
    c2 = lambda i: (0, 0)
    per_b = lambda i: ((i * tm) // S, 0)

    def const(a):
        return pl.BlockSpec(a.shape, c2, pipeline_mode=pl.Buffered(1))

    return pl.pallas_call(
        _back_kernel,
        grid=(T // tm,),
        in_specs=[pl.BlockSpec((tm, D_MODEL), row),
                  pl.BlockSpec((tm, M_WIDTH), row),
                  pl.BlockSpec((tm, M_WIDTH), row),
                  const(w["wout"]), const(w["gx"]), const(w["wqx"]), const(w["xqg"]),
                  pl.BlockSpec((Nm, X_WIDTH), per_b),
                  pl.BlockSpec((Nm, X_WIDTH), per_b),
                  const(w["wox"]), const(w["gf"]), const(w["w1"]), const(w["w2"])],
        out_specs=pl.BlockSpec((tm, D_MODEL), row),
        out_shape=jax.ShapeDtypeStruct((T, D_MODEL), f32),
        compiler_params=pltpu.CompilerParams(
            dimension_semantics=("arbitrary",), vmem_limit_bytes=VMEM_LIMIT),
        name="back",
    )(x, ym, ya, w["wout"], w["gx"], w["wqx"], w["xqg"], kx, vx, w["wox"], w["gf"], w["w1"], w["w2"])


def _rope_pad(a):
    z = jnp.zeros(a.shape[:-1] + (ROPE_DIM // 2,), a.dtype)
    return jnp.concatenate([a[..., :32], z, a[..., 32:], z], axis=-1)


def _block_diag_heads(w):
    H = w.shape[0]
    wp = jnp.pad(w, ((0, 0), (0, 0), (0, M_DKP - M_DK)))
    out = jnp.zeros((H, M_DV, H, M_DKP), w.dtype)
    for hh in range(H):
        out = out.at[hh, :, hh, :].set(wp[hh])
    return out.reshape(H * M_DV, H * M_DKP)


def _layer_weights(l, norm_mix_g, w_in, conv_w, conv_b, wq_m, wk_m, b_igate, b_fgate, m_out_g,
                   cq_norm_g, ckv_norm_g, w_uq, w_ukv, qk_norm_q, qk_norm_k, a_out_g, w_out,
                   norm_x_g, norm_mem_g, wq_x, wkv_x, xq_norm_g, xk_norm_g, wo_x,
                   norm_ffn_g, w_ff1, w_ff2):
    wi = w_in[l]
    o_v, o_o, o_g = M_WIDTH, 2 * M_WIDTH, 3 * M_WIDTH
    o_f = o_g + M_HEADS
    o_cq = o_f + M_HEADS
    o_ckv = o_cq + Q_LORA
    o_kr = o_ckv + KV_LORA
    wnat = jnp.concatenate([wi[:, :o_v], wi[:, o_o:o_g], wi[:, o_cq:o_kr], _rope_pad(wi[:, o_kr:])], axis=1)
    zg = jnp.zeros((8 - M_HEADS, D_MODEL), wi.dtype)
    wtr = jnp.concatenate([wi[:, o_v:o_o].T, wi[:, o_g:o_f].T, zg, wi[:, o_f:o_cq].T, zg], axis=0)
    z4 = jnp.zeros((8 - M_HEADS,), f32)
    gbias = jnp.concatenate([b_igate[l], z4, b_fgate[l], z4])[:, None]
    wuq = w_uq[l].reshape(Q_LORA, A_HEADS, A_QK)
    wuq = jnp.concatenate([wuq[..., :NOPE_DIM], _rope_pad(wuq[..., NOPE_DIM:])], axis=-1)
    wukv = w_ukv[l].reshape(KV_LORA, A_HEADS, NOPE_DIM + A_DV)
    return {
        "gmix": norm_mix_g[l][None, :],
        "wnat": wnat.astype(bf16),
        "wtr": wtr.astype(bf16),
        "convw": conv_w[l],
        "convb": conv_b[l][None, :],
        "wqt": (_block_diag_heads(wq_m[l]) * (M_DK ** -0.5)).T.astype(bf16),
        "wk": _block_diag_heads(wk_m[l]).astype(bf16),
        "gbias": gbias,
        "mog": m_out_g[l].reshape(1, M_WIDTH),
        "cqg": cq_norm_g[l][None, :],
        "ckvg": ckv_norm_g[l][None, :],
        "wuq": wuq.reshape(Q_LORA, A_HEADS * QK_PAD).astype(bf16),
        "wukvk": wukv[..., :NOPE_DIM].reshape(KV_LORA, A_HEADS * NOPE_DIM).astype(bf16),
        "wukvvt": wukv[..., NOPE_DIM:].reshape(KV_LORA, A_HEADS * A_DV).T.astype(bf16),
        "gq": jnp.concatenate([qk_norm_q[l][:NOPE_DIM], _rope_pad(qk_norm_q[l][NOPE_DIM:])])[None, :],
        "gk": jnp.concatenate([qk_norm_k[l][:NOPE_DIM], _rope_pad(qk_norm_k[l][NOPE_DIM:])])[None, :],
        "aog": a_out_g[l].reshape(A_HEADS * A_DV, 1),
        "wout": w_out[l].astype(bf16),
        "gx": norm_x_g[l][None, :],
        "gmem": norm_mem_g[l][None, :],
        "wqx": wq_x[l].astype(bf16),
        "wkvx": wkv_x[l].astype(bf16),
        "xqg": xq_norm_g[l][None, :],
        "xkg": xk_norm_g[l][None, :],
        "wox": wo_x[l].astype(bf16),
        "gf": norm_ffn_g[l][None, :],
        "w1": w_ff1[l].astype(bf16),
        "w2": w_ff2[l].astype(bf16),
    }


def kernel(x, mem, positions, norm_mix_g, w_in, conv_w, conv_b, wq_m, wk_m, b_igate, b_fgate, m_out_g, cq_norm_g, ckv_norm_g, w_uq, w_ukv, qk_norm_q, qk_norm_k, a_out_g, w_out, norm_x_g, norm_mem_g, wq_x, wkv_x, xq_norm_g, xk_norm_g, wo_x, norm_ffn_g, w_ff1, w_ff2):
    B, S, D = x.shape
    Nm = mem.shape[1]
    depth = w_in.shape[0]
    assert D == D_MODEL and S % (2 * SUB) == 0
    tm_front = 2 * SUB
    tm_back = SUB
    params = (norm_mix_g, w_in, conv_w, conv_b, wq_m, wk_m, b_igate, b_fgate, m_out_g,
              cq_norm_g, ckv_norm_g, w_uq, w_ukv, qk_norm_q, qk_norm_k, a_out_g, w_out,
              norm_x_g, norm_mem_g, wq_x, wkv_x, xq_norm_g, xk_norm_g, wo_x,
              norm_ffn_g, w_ff1, w_ff2)

    cos, sin = _rope_tables(positions)
    xt = x.reshape(B * S, D)
    mem2d = mem.reshape(B * Nm, D)
    for l in range(depth):
        w = _layer_weights(l, *params)
        ym, q, k, vt = _front(xt, cos, sin, w, B, S, tm_front)
        ya = _attention(q, k, vt, w["aog"], B, S, ATT_T)
        kx, vx = _memkv(mem2d, w["gmem"], w["wkvx"], w["xkg"], B, Nm)
        xt = _back(xt, ym, ya, kx, vx, w, B, S, Nm, tm_back)
    return xt.reshape(B, S, D)
```

```python
import functools

import jax
import jax.numpy as jnp
from jax import lax
from jax.experimental import pallas as pl
from jax.experimental.pallas import tpu as pltpu

f32 = jnp.float32
bf16 = jnp.bfloat16

EPS = 1e-6
D_MODEL = 1024
M_HEADS = 4
M_WIDTH = 512
M_DV = 128
M_DK = 64
CONV_K = 4
A_HEADS = 4
A_DV = 128
Q_LORA = 256
KV_LORA = 128
NOPE_DIM = 128
ROPE_DIM = 64
A_QK = NOPE_DIM + ROPE_DIM
ROPE_THETA = 10000.0
X_HEADS = 4
X_HD = 128
X_WIDTH = 512
D_FF = 4096
FF_CHUNK = 1024

M_CHUNK = 128
M_DKP = 128
SUB = 512
ATT_T = 256
QK_PAD = 256
ONES_ROWS = 16
N_U, N_O, N_CQ, N_CKV, N_KR, N_END = 0, 512, 1024, 1280, 1408, 1536
T_V, T_GI, T_GF, T_END = 0, 512, 520, 528

VMEM_LIMIT = 56 * 1024 * 1024
LOG2E = 1.4426950408889634


def _dot(a, b):
    return jnp.dot(a, b, preferred_element_type=f32)


def _dot_nt(a, b):
    return lax.dot_general(a, b, (((1,), (1,)), ((), ())), preferred_element_type=f32)


def _rms(x, g, n=None):
    n = x.shape[-1] if n is None else n
    ms = jnp.sum(x * x, axis=-1, keepdims=True) * (1.0 / n)
    return x * lax.rsqrt(ms + EPS) * g


def _sigmoid(x):
    return 1.0 / (1.0 + jnp.exp(-x))


def _log_sigmoid(x):
    return jnp.minimum(x, 0.0) - jnp.log1p(jnp.exp(-jnp.abs(x)))


def _split3(x):
    a = x.astype(bf16)
    r = x - a.astype(f32)
    b = r.astype(bf16)
    c = (r - b.astype(f32)).astype(bf16)
    return a, b, c


def _rope_kernel(pos_ref, inv_ref, sgn_ref, cos_ref, sin_ref):
    ang = pos_ref[...].astype(f32) * inv_ref[...]
    cos_ref[...] = jnp.cos(ang) * jnp.abs(sgn_ref[...])
    sin_ref[...] = jnp.sin(ang) * sgn_ref[...]


def _rope_tables(positions):
    T = positions.size
    tm = 2 * SUB
    assert T % tm == 0
    inv = 1.0 / (ROPE_THETA ** (jnp.arange(0, ROPE_DIM, 2, dtype=f32) / ROPE_DIM))
    z = jnp.zeros((ROPE_DIM // 2,), f32)
    o = jnp.ones((ROPE_DIM // 2,), f32)
    inv_p = jnp.concatenate([inv, z, inv, z])[None, :]
    sgn = jnp.concatenate([-o, z, o, z])[None, :]
    return pl.pallas_call(
        _rope_kernel,
        grid=(T // tm,),
        in_specs=[pl.BlockSpec((tm, 1), lambda i: (i, 0)),
                  pl.BlockSpec((1, 128), lambda i: (0, 0)),
                  pl.BlockSpec((1, 128), lambda i: (0, 0))],
        out_specs=[pl.BlockSpec((tm, 128), lambda i: (i, 0)),
                   pl.BlockSpec((tm, 128), lambda i: (i, 0))],
        out_shape=[jax.ShapeDtypeStruct((T, 128), f32)] * 2,
        name="rope_tables",
    )(positions.reshape(T, 1), inv_p, sgn)


def _front_kernel(x_ref, cos_ref, sin_ref, gmix_ref, wnat_ref, wtr_ref, convw_ref, convb_ref,
                  wqt_ref, wk_ref, gbias_ref, mog_ref, cqg_ref, ckvg_ref,
                  wuq_ref, wukvk_ref, wukvvt_ref, gq_ref, gk_ref,
                  ym_ref, q_ref, k_ref, vt_ref,
                  ubuf_ref, st_ref, mst_ref, lhs_ref, rhs_ref, num_ref, *, tm):
    L = M_CHUNK

    @pl.when(pl.program_id(1) == 0)
    def _():
        ubuf_ref[0:8, :] = jnp.zeros((8, M_WIDTH), f32)
        st_ref[...] = jnp.zeros(st_ref.shape, f32)
        mst_ref[...] = jnp.zeros(mst_ref.shape, f32)

    srow = lax.broadcasted_iota(jnp.int32, (L, L), 0)
    qcol = lax.broadcasted_iota(jnp.int32, (L, L), 1)
    causal_t = srow <= qcol
    triu = jnp.where(causal_t, 1.0, 0.0).astype(bf16)
    ones_f = jnp.ones((ONES_ROWS, L), f32)
    ones_b = jnp.ones((ONES_ROWS, L), bf16)
    zpad = jnp.zeros((L - 8, L), f32)
    qscale = A_QK ** -0.5 * LOG2E
    gq_n, gq_r = gq_ref[:, 0:128] * qscale, gq_ref[:, 128:256] * qscale
    gk_n, gk_r = gk_ref[:, 0:128], gk_ref[:, 128:256]
    m_run = mst_ref[:, 0:1]

    def project(sub):
        h = _rms(x_ref[sub * SUB:(sub + 1) * SUB, :], gmix_ref[...]).astype(bf16)
        pn = _dot(h, wnat_ref[...])
        pt = _dot_nt(wtr_ref[...], h)
        return pn, pt

    def prepare(sub, pn, pt):
        rows = slice(sub * SUB, (sub + 1) * SUB)

        u = pn[:, N_U:N_U + M_WIDTH]
        ubuf_ref[8:8 + SUB, :] = u
        acc = convb_ref[...] + convw_ref[CONV_K - 1:CONV_K, :] * u
        for j in range(CONV_K - 1):
            off = 8 - (CONV_K - 1) + j
            acc = acc + convw_ref[j:j + 1, :] * ubuf_ref[off:off + SUB, :]
        ubuf_ref[0:8, :] = u[SUB - 8:SUB, :]
        ucb = (acc * _sigmoid(acc)).astype(bf16)

        cos = cos_ref[rows, :]
        sin = sin_ref[rows, :]

        def rope(t):
            return t * cos + pltpu.roll(t, 64, 1) * sin

        cqn = _rms(pn[:, N_CQ:N_CQ + Q_LORA], cqg_ref[...]).astype(bf16)
        qa = _dot(cqn, wuq_ref[...])
        ckvn = _rms(pn[:, N_CKV:N_CKV + KV_LORA], ckvg_ref[...]).astype(bf16)
        kn = _dot(ckvn, wukvk_ref[...])
        vt = _dot_nt(wukvvt_ref[...], ckvn).astype(bf16)
        for c in range(SUB // ATT_T):
            blk = sub * (SUB // ATT_T) + c
            vt_ref[blk * A_HEADS * A_DV:(blk + 1) * A_HEADS * A_DV, :] = vt[:, c * ATT_T:(c + 1) * ATT_T]
        kr = rope(_rms(pn[:, N_KR:N_KR + 128], gk_r, ROPE_DIM)).astype(bf16)
        for hh in range(A_HEADS):
            c0 = hh * QK_PAD
            qn = _rms(qa[:, c0:c0 + 128], gq_n)
            qr = rope(_rms(qa[:, c0 + 128:c0 + 256], gq_r, ROPE_DIM))
            q_ref[rows, c0:c0 + 128] = qn.astype(bf16)
            q_ref[rows, c0 + 128:c0 + 256] = qr.astype(bf16)
            k_ref[rows, c0:c0 + 128] = _rms(kn[:, hh * 128:(hh + 1) * 128], gk_n).astype(bf16)
            k_ref[rows, c0 + 128:c0 + 256] = kr

        qt_all = _dot_nt(wqt_ref[...], ucb).astype(bf16)
        k_all = _dot(ucb, wk_ref[...]).astype(bf16)
        gi = pt[T_GI:T_GI + 8, :] + gbias_ref[0:8, :]
        logf = _log_sigmoid(pt[T_GF:T_GF + 8, :] + gbias_ref[8:16, :])
        return pn, pt, qt_all, k_all, gi, logf

    def recur_states(vals, m_run, states):
        pn, pt, qt_all, k_all, gi, logf = vals
        work = []
        for c in range(SUB // L):
            cols = slice(c * L, (c + 1) * L)
            bsum = _dot(jnp.concatenate(_split3(logf[:, cols]), axis=0), triu)
            b = bsum[0:8] + bsum[8:16] + bsum[16:24]
            ic = gi[:, cols]
            btot = b[:, L - 1:L]
            a = btot - b + ic
            mloc = jnp.max(a, axis=1, keepdims=True)
            wa = jnp.exp(a - mloc)
            m_prev = m_run
            m_run = jnp.maximum(btot + m_prev, mloc)
            s_old = jnp.exp(btot + m_prev - m_run)
            s_loc = jnp.exp(mloc - m_run)
            r_t = jnp.concatenate([ic - b, zpad], axis=0).T
            inter_all = b + m_prev
            for hh in range(M_HEADS):
                k_c = k_all[c * L:(c + 1) * L, hh * M_DKP:(hh + 1) * M_DKP]
                qt_c = qt_all[hh * M_DKP:(hh + 1) * M_DKP, cols]
                vt_c = pt[T_V + hh * M_DV:T_V + (hh + 1) * M_DV, cols]
                st = states[hh]
                sc = _dot(k_c, qt_c)
                vt_w = jnp.concatenate([vt_c, ones_f], axis=0) * wa[hh:hh + 1, :]
                loc = _dot(vt_w.astype(bf16), k_c)
                states[hh] = s_old[hh:hh + 1, :] * st + s_loc[hh:hh + 1, :] * loc
                work.append((c, hh, sc, st, qt_c, vt_c, r_t[:, hh:hh + 1], b[hh:hh + 1, :],
                             inter_all[hh:hh + 1, :]))
        return work, m_run

    def recur_outputs(sub, pn, work):
        mjs = []
        for i, (c, hh, sc, st, qt_c, vt_c, r_col, b_row, inter) in enumerate(work):
            dlog = jnp.where(causal_t, r_col + b_row, -jnp.inf)
            mj = jnp.maximum(inter, jnp.max(dlog, axis=0, keepdims=True))
            qk = (sc * jnp.exp(dlog - mj)).astype(bf16)
            s_int = jnp.exp(inter - mj)
            rhs_ref[i] = jnp.concatenate([(qt_c.astype(f32) * s_int).astype(bf16), qk], axis=0)
            lhs_ref[i] = jnp.concatenate(
                [st.astype(bf16), jnp.concatenate([vt_c.astype(bf16), ones_b], axis=0)], axis=1)
            mjs.append(mj)
        for i in range(len(work)):
            num_ref[i] = _dot(lhs_ref[i], rhs_ref[i])
        for i, (c, hh, sc, st, qt_c, vt_c, r_col, b_row, inter) in enumerate(work):
            hs = slice(hh * M_DV, (hh + 1) * M_DV)
            crow = slice(sub * SUB + c * L, sub * SUB + (c + 1) * L)
            num = num_ref[i]
            den = num[M_DV:M_DV + 1, :]
            hm = num[0:M_DV, :] / jnp.maximum(jnp.abs(den), jnp.exp(-mjs[i]))
            hn = hm * lax.rsqrt(jnp.sum(hm * hm, axis=0, keepdims=True) * (1.0 / M_DV) + EPS)
            y = hn.T * mog_ref[:, hs] * _sigmoid(pn[c * L:(c + 1) * L, N_O + hh * M_DV:N_O + (hh + 1) * M_DV])
            ym_ref[crow, hs] = y.astype(bf16)

    states = [st_ref[hh] for hh in range(M_HEADS)]
    nsub = tm // SUB
    pn, pt = project(0)
    vals = prepare(0, pn, pt)
    for sub in range(nsub):
        cur = vals
        if sub + 1 < nsub:
            pn, pt = project(sub + 1)
            vals = prepare(sub + 1, pn, pt)
        work, m_run = recur_states(cur, m_run, states)
        recur_outputs(sub, cur[0], work)
    for hh in range(M_HEADS):
        st_ref[hh] = states[hh]
    mst_ref[...] = jnp.broadcast_to(m_run, (8, 128))


def _front(x, cos, sin, w, B, S, tm):
    T = B * S
    nt = S // tm
    vt_rows = (tm // ATT_T) * A_HEADS * A_DV
    nch = (SUB // M_CHUNK) * M_HEADS
    row = lambda b, i: (b * nt + i, 0)
    c2 = lambda b, i: (0, 0)

    def full(a):
        return pl.BlockSpec(a.shape, c2)

    weights = [w["gmix"], w["wnat"], w["wtr"], w["convw"], w["convb"], w["wqt"], w["wk"], w["gbias"],
               w["mog"], w["cqg"], w["ckvg"], w["wuq"], w["wukvk"], w["wukvvt"], w["gq"], w["gk"]]
    return pl.pallas_call(
        functools.partial(_front_kernel, tm=tm),
        grid=(B, nt),
        in_specs=[pl.BlockSpec((tm, D_MODEL), row),
                  pl.BlockSpec((tm, 128), row),
                  pl.BlockSpec((tm, 128), row)] + [full(a) for a in weights],
        out_specs=[pl.BlockSpec((tm, M_WIDTH), row),
                   pl.BlockSpec((tm, A_HEADS * QK_PAD), row),
                   pl.BlockSpec((tm, A_HEADS * QK_PAD), row),
                   pl.BlockSpec((vt_rows, ATT_T), row)],
        out_shape=[jax.ShapeDtypeStruct((T, M_WIDTH), bf16),
                   jax.ShapeDtypeStruct((T, A_HEADS * QK_PAD), bf16),
                   jax.ShapeDtypeStruct((T, A_HEADS * QK_PAD), bf16),
                   jax.ShapeDtypeStruct((B * nt * vt_rows, ATT_T), bf16)],
        scratch_shapes=[pltpu.VMEM((SUB + 8, M_WIDTH), f32),
                        pltpu.VMEM((M_HEADS, M_DV + ONES_ROWS, M_DKP), f32),
                        pltpu.VMEM((8, 128), f32),
                        pltpu.VMEM((nch, M_DV + ONES_ROWS, M_DKP + M_CHUNK), bf16),
                        pltpu.VMEM((nch, M_DKP + M_CHUNK, M_CHUNK), bf16),
                        pltpu.VMEM((nch, M_DV + ONES_ROWS, M_CHUNK), f32)],
        compiler_params=pltpu.CompilerParams(
            dimension_semantics=("arbitrary", "arbitrary"), vmem_limit_bytes=VMEM_LIMIT),
        name="front",
    )(x, cos, sin, *weights)


def _attn_kernel(q_ref, k_ref, vt_ref, g_ref, o_ref, acc_ref, p_ref, *, tq):
    qi = pl.program_id(1)
    ones = jnp.ones((ONES_ROWS, tq), bf16)
    acc_ref[...] = jnp.zeros(acc_ref.shape, f32)

    def block(j, ms, masked):
        r0 = pl.multiple_of(j * tq, tq)
        v0 = j * (A_HEADS * A_DV)
        ss = [_dot_nt(k_ref[pl.ds(r0, tq), hh * QK_PAD:(hh + 1) * QK_PAD],
                      q_ref[:, hh * QK_PAD:(hh + 1) * QK_PAD]) for hh in range(A_HEADS)]
        out, alphas = [], []
        for hh in range(A_HEADS):
            s = ss[hh]
            if masked:
                row = lax.broadcasted_iota(jnp.int32, (tq, tq), 0)
                col = lax.broadcasted_iota(jnp.int32, (tq, tq), 1)
                s = jnp.where(row <= col, s, -jnp.inf)
            m_new = jnp.maximum(ms[hh], jnp.max(s, axis=0, keepdims=True))
            alphas.append(jnp.exp2(ms[hh] - m_new))
            p_ref[hh] = jnp.exp2(s - m_new).astype(bf16)
            out.append(m_new)
        for hh in range(A_HEADS):
            vte = jnp.concatenate(
                [vt_ref[pl.ds(pl.multiple_of(v0 + hh * A_DV, A_DV), A_DV), :], ones], axis=0)
            acc_ref[hh] = alphas[hh] * acc_ref[hh] + _dot(vte, p_ref[hh])
        return tuple(out)

    ms = tuple(jnp.full((1, tq), -jnp.inf, f32) for _ in range(A_HEADS))
    ms = lax.fori_loop(0, qi, lambda j, c: block(j, c, False), ms)
    block(qi, ms, True)
    for hh in range(A_HEADS):
        a = acc_ref[hh]
        o = a[0:A_DV, :] / a[A_DV:A_DV + 1, :]
        ms2 = jnp.sum(o * o, axis=0, keepdims=True) * (1.0 / A_DV)
        y = o * lax.rsqrt(ms2 + EPS) * g_ref[hh * A_DV:(hh + 1) * A_DV, :]
        o_ref[:, hh * A_DV:(hh + 1) * A_DV] = y.T.astype(bf16)


def _attention(q, k, vt, g, B, S, tq):
    T = B * S
    nq = S // tq
    return pl.pallas_call(
        functools.partial(_attn_kernel, tq=tq),
        grid=(B, nq),
        in_specs=[pl.BlockSpec((tq, A_HEADS * QK_PAD), lambda b, i: (b * nq + i, 0)),
                  pl.BlockSpec((S, A_HEADS * QK_PAD), lambda b, i: (b, 0)),
                  pl.BlockSpec((nq * A_HEADS * A_DV, tq), lambda b, i: (b, 0)),
                  pl.BlockSpec((A_HEADS * A_DV, 1), lambda b, i: (0, 0))],
        out_specs=pl.BlockSpec((tq, A_HEADS * A_DV), lambda b, i: (b * nq + i, 0)),
        out_shape=jax.ShapeDtypeStruct((T, A_HEADS * A_DV), bf16),
        scratch_shapes=[pltpu.VMEM((A_HEADS, A_DV + ONES_ROWS, tq), f32),
                        pltpu.VMEM((A_HEADS, tq, tq), bf16)],
        compiler_params=pltpu.CompilerParams(
            dimension_semantics=("arbitrary", "arbitrary"), vmem_limit_bytes=VMEM_LIMIT),
        name="mla_attention",
    )(q, k, vt, g)


def _memkv_kernel(mem_ref, g_ref, w_ref, kg_ref, k_ref, v_ref):
    mn = _rms(mem_ref[...], g_ref[...]).astype(bf16)
    kv = _dot(mn, w_ref[...])
    for hh in range(X_HEADS):
        sl = slice(hh * X_HD, (hh + 1) * X_HD)
        k_ref[:, sl] = _rms(kv[:, sl], kg_ref[...]).astype(bf16)
    v_ref[...] = kv[:, X_WIDTH:].astype(bf16)


def _memkv(mem2d, g, wkv, kg, B, Nm):
    return pl.pallas_call(
        _memkv_kernel,
        grid=(B,),
        in_specs=[pl.BlockSpec((Nm, D_MODEL), lambda b: (b, 0)),
                  pl.BlockSpec((1, D_MODEL), lambda b: (0, 0)),
                  pl.BlockSpec((D_MODEL, 2 * X_WIDTH), lambda b: (0, 0)),
                  pl.BlockSpec((1, X_HD), lambda b: (0, 0))],
        out_specs=[pl.BlockSpec((Nm, X_WIDTH), lambda b: (b, 0)),
                   pl.BlockSpec((Nm, X_WIDTH), lambda b: (b, 0))],
        out_shape=[jax.ShapeDtypeStruct((B * Nm, X_WIDTH), bf16)] * 2,
        compiler_params=pltpu.CompilerParams(
            dimension_semantics=("arbitrary",), vmem_limit_bytes=VMEM_LIMIT),
        name="mem_kv",
    )(mem2d, g, wkv, kg)


def _back_kernel(x_ref, ym_ref, ya_ref, wout_ref, gx_ref, wqx_ref, xqg_ref, kx_ref, vx_ref,
                 wox_ref, gf_ref, w1_ref, w2_ref, out_ref):
    x1 = (x_ref[...] + _dot(ym_ref[...], wout_ref[0:M_WIDTH, :])
          + _dot(ya_ref[...], wout_ref[M_WIDTH:2 * M_WIDTH, :]))

    hn = _rms(x1, gx_ref[...]).astype(bf16)
    qx = _dot(hn, wqx_ref[...])
    heads = []
    for hh in range(X_HEADS):
        sl = slice(hh * X_HD, (hh + 1) * X_HD)
        qh = (_rms(qx[:, sl], xqg_ref[...]) * (X_HD ** -0.5)).astype(bf16)
        s = _dot_nt(qh, kx_ref[:, sl])
        p = jnp.exp(s - jnp.max(s, axis=-1, keepdims=True))
        o = _dot(p.astype(bf16), vx_ref[:, sl]) / jnp.sum(p, axis=-1, keepdims=True)
        heads.append(o.astype(bf16))
    x2 = x1 + _dot(jnp.concatenate(heads, axis=-1), wox_ref[...])

    hf = _rms(x2, gf_ref[...]).astype(bf16)
    acc = x2
    for c in range(D_FF // FF_CHUNK):
        sl = slice(c * FF_CHUNK, (c + 1) * FF_CHUNK)
        a = jnp.maximum(_dot(hf, w1_ref[:, sl]), 0.0)
        acc = acc + _dot((a * a).astype(bf16), w2_ref[sl, :])
    out_ref[...] = acc


def _back(x, ym, ya, kx, vx, w, B, S, Nm, tm):
    T = B * S
    row = lambda i: (i, 0)
    c2 = lambda i: (0, 0)
    per_b = lambda i: ((i * tm) // S, 0)

    def const(a):
        return pl.BlockSpec(a.shape, c2, pipeline_mode=pl.Buffered(1))

    return pl.pallas_call(
        _back_kernel,
        grid=(T // tm,),
        in_specs=[pl.BlockSpec((tm, D_MODEL), row),
                  pl.BlockSpec((tm, M_WIDTH), row),
                  pl.BlockSpec((tm, M_WIDTH), row),
                  const(w["wout"]), const(w["gx"]), const(w["wqx"]), const(w["xqg"]),
                  pl.BlockSpec((Nm, X_WIDTH), per_b),
                  pl.BlockSpec((Nm, X_WIDTH), per_b),
                  const(w["wox"]), const(w["gf"]), const(w["w1"]), const(w["w2"])],
        out_specs=pl.BlockSpec((tm, D_MODEL), row),
        out_shape=jax.ShapeDtypeStruct((T, D_MODEL), f32),
        compiler_params=pltpu.CompilerParams(
            dimension_semantics=("arbitrary",), vmem_limit_bytes=VMEM_LIMIT),
        name="back",
    )(x, ym, ya, w["wout"], w["gx"], w["wqx"], w["xqg"], kx, vx, w["wox"], w["gf"], w["w1"], w["w2"])


def _rope_pad(a):
    z = jnp.zeros(a.shape[:-1] + (ROPE_DIM // 2,), a.dtype)
    return jnp.concatenate([a[..., :32], z, a[..., 32:], z], axis=-1)


def _block_diag_heads(w):
    H = w.shape[0]
    wp = jnp.pad(w, ((0, 0), (0, 0), (0, M_DKP - M_DK)))
    out = jnp.zeros((H, M_DV, H, M_DKP), w.dtype)
    for hh in range(H):
        out = out.at[hh, :, hh, :].set(wp[hh])
    return out.reshape(H * M_DV, H * M_DKP)


def _layer_weights(l, norm_mix_g, w_in, conv_w, conv_b, wq_m, wk_m, b_igate, b_fgate, m_out_g,
                   cq_norm_g, ckv_norm_g, w_uq, w_ukv, qk_norm_q, qk_norm_k, a_out_g, w_out,
                   norm_x_g, norm_mem_g, wq_x, wkv_x, xq_norm_g, xk_norm_g, wo_x,
                   norm_ffn_g, w_ff1, w_ff2):
    wi = w_in[l]
    o_v, o_o, o_g = M_WIDTH, 2 * M_WIDTH, 3 * M_WIDTH
    o_f = o_g + M_HEADS
    o_cq = o_f + M_HEADS
    o_ckv = o_cq + Q_LORA
    o_kr = o_ckv + KV_LORA
    wnat = jnp.concatenate([wi[:, :o_v], wi[:, o_o:o_g], wi[:, o_cq:o_kr], _rope_pad(wi[:, o_kr:])], axis=1)
    zg = jnp.zeros((8 - M_HEADS, D_MODEL), wi.dtype)
    wtr = jnp.concatenate([wi[:, o_v:o_o].T, wi[:, o_g:o_f].T, zg, wi[:, o_f:o_cq].T, zg], axis=0)
    z4 = jnp.zeros((8 - M_HEADS,), f32)
    gbias = jnp.concatenate([b_igate[l], z4, b_fgate[l], z4])[:, None]
    wuq = w_uq[l].reshape(Q_LORA, A_HEADS, A_QK)
    wuq = jnp.concatenate([wuq[..., :NOPE_DIM], _rope_pad(wuq[..., NOPE_DIM:])], axis=-1)
    wukv = w_ukv[l].reshape(KV_LORA, A_HEADS, NOPE_DIM + A_DV)
    return {
        "gmix": norm_mix_g[l][None, :],
        "wnat": wnat.astype(bf16),
        "wtr": wtr.astype(bf16),
        "convw": conv_w[l],
        "convb": conv_b[l][None, :],
        "wqt": (_block_diag_heads(wq_m[l]) * (M_DK ** -0.5)).T.astype(bf16),
        "wk": _block_diag_heads(wk_m[l]).astype(bf16),
        "gbias": gbias,
        "mog": m_out_g[l].reshape(1, M_WIDTH),
        "cqg": cq_norm_g[l][None, :],
        "ckvg": ckv_norm_g[l][None, :],
        "wuq": wuq.reshape(Q_LORA, A_HEADS * QK_PAD).astype(bf16),
        "wukvk": wukv[..., :NOPE_DIM].reshape(KV_LORA, A_HEADS * NOPE_DIM).astype(bf16),
        "wukvvt": wukv[..., NOPE_DIM:].reshape(KV_LORA, A_HEADS * A_DV).T.astype(bf16),
        "gq": jnp.concatenate([qk_norm_q[l][:NOPE_DIM], _rope_pad(qk_norm_q[l][NOPE_DIM:])])[None, :],
        "gk": jnp.concatenate([qk_norm_k[l][:NOPE_DIM], _rope_pad(qk_norm_k[l][NOPE_DIM:])])[None, :],
        "aog": a_out_g[l].reshape(A_HEADS * A_DV, 1),
        "wout": w_out[l].astype(bf16),
        "gx": norm_x_g[l][None, :],
        "gmem": norm_mem_g[l][None, :],
        "wqx": wq_x[l].astype(bf16),
        "wkvx": wkv_x[l].astype(bf16),
        "xqg": xq_norm_g[l][None, :],
        "xkg": xk_norm_g[l][None, :],
        "wox": wo_x[l].astype(bf16),
        "gf": norm_ffn_g[l][None, :],
        "w1": w_ff1[l].astype(bf16),
        "w2": w_ff2[l].astype(bf16),
    }


def kernel(x, mem, positions, norm_mix_g, w_in, conv_w, conv_b, wq_m, wk_m, b_igate, b_fgate, m_out_g, cq_norm_g, ckv_norm_g, w_uq, w_ukv, qk_norm_q, qk_norm_k, a_out_g, w_out, norm_x_g, norm_mem_g, wq_x, wkv_x, xq_norm_g, xk_norm_g, wo_x, norm_ffn_g, w_ff1, w_ff2):
    B, S, D = x.shape
    Nm = mem.shape[1]
    depth = w_in.shape[0]
    assert D == D_MODEL and S % (2 * SUB) == 0
    tm_front = 2 * SUB
    tm_back = SUB
    params = (norm_mix_g, w_in, conv_w, conv_b, wq_m, wk_m, b_igate, b_fgate, m_out_g,
              cq_norm_g, ckv_norm_g, w_uq, w_ukv, qk_norm_q, qk_norm_k, a_out_g, w_out,
              norm_x_g, norm_mem_g, wq_x, wkv_x, xq_norm_g, xk_norm_g, wo_x,
              norm_ffn_g, w_ff1, w_ff2)

    cos, sin = _rope_tables(positions)
    xt = x.reshape(B * S, D)
    mem2d = mem.reshape(B * Nm, D)
    for l in range(depth):
        w = _layer_weights(l, *params)
        ym, q, k, vt = _front(xt, cos, sin, w, B, S, tm_front)
        ya = _attention(q, k, vt, w["aog"], B, S, ATT_T)
        kx, vx = _memkv(mem2d, w["gmem"], w["wkvx"], w["xkg"], B, Nm)
        xt = _back(xt, ym, ya, kx, vx, w, B, S, Nm, tm_back)
    return xt.reshape(B, S, D)
```

```python
import functools

import jax
import jax.numpy as jnp
from jax import lax
from jax.experimental import pallas as pl
from jax.experimental.pallas import tpu as pltpu

f32 = jnp.float32
bf16 = jnp.bfloat16

EPS = 1e-6
D_MODEL = 1024
M_HEADS = 4
M_WIDTH = 512
M_DV = 128
M_DK = 64
CONV_K = 4
A_HEADS = 4
A_DV = 128
Q_LORA = 256
KV_LORA = 128
NOPE_DIM = 128
ROPE_DIM = 64
A_QK = NOPE_DIM + ROPE_DIM
ROPE_THETA = 10000.0
X_HEADS = 4
X_HD = 128
X_WIDTH = 512
D_FF = 4096
FF_CHUNK = 1024

M_CHUNK = 128
M_DKP = 128
SUB = 512
ATT_T = 256
QK_PAD = 256
ONES_ROWS = 16
N_U, N_O, N_CQ, N_CKV, N_KR, N_END = 0, 512, 1024, 1280, 1408, 1536
T_V, T_GI, T_GF, T_END = 0, 512, 520, 528

VMEM_LIMIT = 56 * 1024 * 1024
LOG2E = 1.4426950408889634


def _dot(a, b):
    return jnp.dot(a, b, preferred_element_type=f32)


def _dot_nt(a, b):
    return lax.dot_general(a, b, (((1,), (1,)), ((), ())), preferred_element_type=f32)


def _rms(x, g, n=None):
    n = x.shape[-1] if n is None else n
    ms = jnp.sum(x * x, axis=-1, keepdims=True) * (1.0 / n)
    return x * lax.rsqrt(ms + EPS) * g


def _sigmoid(x):
    return 1.0 / (1.0 + jnp.exp(-x))


def _log_sigmoid(x):
    return jnp.minimum(x, 0.0) - jnp.log1p(jnp.exp(-jnp.abs(x)))


def _split3(x):
    a = x.astype(bf16)
    r = x - a.astype(f32)
    b = r.astype(bf16)
    c = (r - b.astype(f32)).astype(bf16)
    return a, b, c


def _rope_kernel(pos_ref, inv_ref, sgn_ref, cos_ref, sin_ref):
    ang = pos_ref[...].astype(f32) * inv_ref[...]
    cos_ref[...] = jnp.cos(ang) * jnp.abs(sgn_ref[...])
    sin_ref[...] = jnp.sin(ang) * sgn_ref[...]


def _rope_tables(positions):
    T = positions.size
    tm = 2 * SUB
    assert T % tm == 0
    inv = 1.0 / (ROPE_THETA ** (jnp.arange(0, ROPE_DIM, 2, dtype=f32) / ROPE_DIM))
    z = jnp.zeros((ROPE_DIM // 2,), f32)
    o = jnp.ones((ROPE_DIM // 2,), f32)
    inv_p = jnp.concatenate([inv, z, inv, z])[None, :]
    sgn = jnp.concatenate([-o, z, o, z])[None, :]
    return pl.pallas_call(
        _rope_kernel,
        grid=(T // tm,),
        in_specs=[pl.BlockSpec((tm, 1), lambda i: (i, 0)),
                  pl.BlockSpec((1, 128), lambda i: (0, 0)),
                  pl.BlockSpec((1, 128), lambda i: (0, 0))],
        out_specs=[pl.BlockSpec((tm, 128), lambda i: (i, 0)),
                   pl.BlockSpec((tm, 128), lambda i: (i, 0))],
        out_shape=[jax.ShapeDtypeStruct((T, 128), f32)] * 2,
        name="rope_tables",
    )(positions.reshape(T, 1), inv_p, sgn)


def _front_kernel(x_ref, cos_ref, sin_ref, gmix_ref, wnat_ref, wtr_ref, convw_ref, convb_ref,
                  wqt_ref, wk_ref, gbias_ref, mog_ref, cqg_ref, ckvg_ref,
                  wuq_ref, wukvk_ref, wukvvt_ref, gq_ref, gk_ref,
                  ym_ref, q_ref, k_ref, vt_ref,
                  ubuf_ref, st_ref, mst_ref, lhs_ref, rhs_ref, num_ref, mj_ref, gate_ref,
                  sog_ref, spt_ref, sqt_ref, sk_ref, sqa_ref, skn_ref, skr_ref, *, nt):
    L = M_CHUNK
    NC = SUB // L
    g = pl.program_id(0)

    @pl.when(g == 0)
    def _():
        for ref in (ubuf_ref, st_ref, mst_ref, num_ref, mj_ref, gate_ref,
                    sog_ref, spt_ref, sqt_ref, sk_ref, sqa_ref, skn_ref, skr_ref):
            ref[...] = jnp.zeros(ref.shape, ref.dtype)

    keep_conv = jnp.where(g % nt != 0, 1.0, 0.0).astype(f32)
    keep_state = jnp.where((g + nt - 1) % nt != 0, 1.0, 0.0).astype(f32)

    srow = lax.broadcasted_iota(jnp.int32, (L, L), 0)
    qcol = lax.broadcasted_iota(jnp.int32, (L, L), 1)
    causal_t = srow <= qcol
    triu = jnp.where(causal_t, 1.0, 0.0).astype(bf16)
    ones_f = jnp.ones((ONES_ROWS, L), f32)
    ones_b = jnp.ones((ONES_ROWS, L), bf16)
    zpad = jnp.zeros((L - 8, L), f32)
    items = [(c, hh) for c in range(NC) for hh in range(M_HEADS)]

    h = _rms(x_ref[...], gmix_ref[...]).astype(bf16)
    pn = _dot(h, wnat_ref[...])
    pt = _dot_nt(wtr_ref[...], h)

    def slices(c, hh):
        cols = slice(c * L, (c + 1) * L)
        return (sk_ref[c * L:(c + 1) * L, hh * M_DKP:(hh + 1) * M_DKP],
                sqt_ref[hh * M_DKP:(hh + 1) * M_DKP, cols],
                spt_ref[T_V + hh * M_DV:T_V + (hh + 1) * M_DV, cols])

    scores = [_dot(*slices(c, hh)[0:2]) for c, hh in items]
    gi = spt_ref[T_GI:T_GI + 8, :] + gbias_ref[0:8, :]
    logf = _log_sigmoid(spt_ref[T_GF:T_GF + 8, :] + gbias_ref[8:16, :])
    m_run = mst_ref[:, 0:1] * keep_state
    gts = []
    for c in range(NC):
        cols = slice(c * L, (c + 1) * L)
        bsum = _dot(jnp.concatenate(_split3(logf[:, cols]), axis=0), triu)
        b = bsum[0:8] + bsum[8:16] + bsum[16:24]
        ic = gi[:, cols]
        btot = b[:, L - 1:L]
        a = btot - b + ic
        mloc = jnp.max(a, axis=1, keepdims=True)
        wa = jnp.exp(a - mloc)
        m_prev = m_run
        m_run = jnp.maximum(btot + m_prev, mloc)
        s_old = jnp.exp(btot + m_prev - m_run)
        s_loc = jnp.exp(mloc - m_run)
        r_t = jnp.concatenate([ic - b, zpad], axis=0).T
        gts.append((b, wa, s_old, s_loc, r_t, b + m_prev))
    mst_ref[...] = jnp.broadcast_to(m_run, (8, 128))

    cqn = _rms(pn[:, N_CQ:N_CQ + Q_LORA], cqg_ref[...]).astype(bf16)
    qa = _dot(cqn, wuq_ref[...])
    ckvn = _rms(pn[:, N_CKV:N_CKV + KV_LORA], ckvg_ref[...]).astype(bf16)
    kn = _dot(ckvn, wukvk_ref[...])
    vt = _dot_nt(wukvvt_ref[...], ckvn).astype(bf16)
    for c in range(SUB // ATT_T):
        vt_ref[c * A_HEADS * A_DV:(c + 1) * A_HEADS * A_DV, :] = vt[:, c * ATT_T:(c + 1) * ATT_T]
    u = pn[:, N_U:N_U + M_WIDTH]
    ubuf_ref[0:8, :] = ubuf_ref[0:8, :] * keep_conv
    ubuf_ref[8:8 + SUB, :] = u
    acc = convb_ref[...] + convw_ref[CONV_K - 1:CONV_K, :] * u
    for j in range(CONV_K - 1):
        off = 8 - (CONV_K - 1) + j
        acc = acc + convw_ref[j:j + 1, :] * ubuf_ref[off:off + SUB, :]
    ubuf_ref[0:8, :] = u[SUB - 8:SUB, :]
    ucb = (acc * _sigmoid(acc)).astype(bf16)
    qt_all = _dot_nt(wqt_ref[...], ucb).astype(bf16)
    k_all = _dot(ucb, wk_ref[...]).astype(bf16)

    for i, (c, hh) in enumerate(items):
        hs = slice(hh * M_DV, (hh + 1) * M_DV)
        num = num_ref[i]
        den = num[M_DV:M_DV + 1, :]
        hm = num[0:M_DV, :] / jnp.maximum(jnp.abs(den), jnp.exp(-mj_ref[i][0:1, :]))
        hn = hm * lax.rsqrt(jnp.sum(hm * hm, axis=0, keepdims=True) * (1.0 / M_DV) + EPS)
        ym_ref[c * L:(c + 1) * L, hs] = (hn.T * gate_ref[c * L:(c + 1) * L, hs]).astype(bf16)

    states = [st_ref[hh] * keep_state for hh in range(M_HEADS)]
    for i, (c, hh) in enumerate(items):
        b, wa, s_old, s_loc, r_t, inter_all = gts[c]
        k_c, qt_c, vt_c = slices(c, hh)
        st = states[hh]
        vt_w = jnp.concatenate([vt_c, ones_f], axis=0) * wa[hh:hh + 1, :]
        loc = _dot(vt_w.astype(bf16), k_c)
        states[hh] = s_old[hh:hh + 1, :] * st + s_loc[hh:hh + 1, :] * loc
        inter = inter_all[hh:hh + 1, :]
        dlog = jnp.where(causal_t, r_t[:, hh:hh + 1] + b[hh:hh + 1, :], -jnp.inf)
        mj = jnp.maximum(inter, jnp.max(dlog, axis=0, keepdims=True))
        qk = (scores[i] * jnp.exp(dlog - mj)).astype(bf16)
        s_int = jnp.exp(inter - mj)
        rhs_ref[i] = jnp.concatenate([(qt_c.astype(f32) * s_int).astype(bf16), qk], axis=0)
        lhs_ref[i] = jnp.concatenate(
            [st.astype(bf16), jnp.concatenate([vt_c.astype(bf16), ones_b], axis=0)], axis=1)
        mj_ref[i] = jnp.broadcast_to(mj, (8, L))
    for hh in range(M_HEADS):
        st_ref[hh] = states[hh]
    for i in range(len(items)):
        num_ref[i] = _dot(lhs_ref[i], rhs_ref[i])
    gate_ref[...] = mog_ref[...] * _sigmoid(sog_ref[...])

    cos = cos_ref[...]
    sin = sin_ref[...]

    def rope(t):
        return t * cos + pltpu.roll(t, 64, 1) * sin

    qscale = A_QK ** -0.5 * LOG2E
    gq_n, gq_r = gq_ref[:, 0:128] * qscale, gq_ref[:, 128:256] * qscale
    gk_n, gk_r = gk_ref[:, 0:128], gk_ref[:, 128:256]
    kr = rope(_rms(skr_ref[...], gk_r, ROPE_DIM)).astype(bf16)
    for hh in range(A_HEADS):
        c0 = hh * QK_PAD
        qn = _rms(sqa_ref[:, c0:c0 + 128], gq_n)
        qr = rope(_rms(sqa_ref[:, c0 + 128:c0 + 256], gq_r, ROPE_DIM))
        q_ref[:, c0:c0 + 128] = qn.astype(bf16)
        q_ref[:, c0 + 128:c0 + 256] = qr.astype(bf16)
        k_ref[:, c0:c0 + 128] = _rms(skn_ref[:, hh * 128:(hh + 1) * 128], gk_n).astype(bf16)
        k_ref[:, c0 + 128:c0 + 256] = kr

    sog_ref[...] = pn[:, N_O:N_O + M_WIDTH]
    spt_ref[...] = pt
    sqt_ref[...] = qt_all
    sk_ref[...] = k_all
    sqa_ref[...] = qa
    skn_ref[...] = kn
    skr_ref[...] = pn[:, N_KR:N_KR + 128]


def _front(x, cos, sin, w, B, S):
    tm = SUB
    nt = S // tm
    G = B * nt
    nch = (SUB // M_CHUNK) * M_HEADS
    vt_rows = (tm // ATT_T) * A_HEADS * A_DV

    def lag(d):
        return lambda g: (jnp.clip(g - d, 0, G - 1), 0)

    c2 = lambda g: (0, 0)

    def full(a):
        return pl.BlockSpec(a.shape, c2)

    weights = [w["gmix"], w["wnat"], w["wtr"], w["convw"], w["convb"], w["wqt"], w["wk"], w["gbias"],
               w["mog"], w["cqg"], w["ckvg"], w["wuq"], w["wukvk"], w["wukvvt"], w["gq"], w["gk"]]
    return pl.pallas_call(
        functools.partial(_front_kernel, nt=nt),
        grid=(G + 2,),
        in_specs=[pl.BlockSpec((tm, D_MODEL), lag(0)),
                  pl.BlockSpec((tm, 128), lag(1)),
                  pl.BlockSpec((tm, 128), lag(1))] + [full(a) for a in weights],
        out_specs=[pl.BlockSpec((tm, M_WIDTH), lag(2)),
                   pl.BlockSpec((tm, A_HEADS * QK_PAD), lag(1)),
                   pl.BlockSpec((tm, A_HEADS * QK_PAD), lag(1)),
                   pl.BlockSpec((vt_rows, ATT_T), lag(0))],
        out_shape=[jax.ShapeDtypeStruct((G * tm, M_WIDTH), bf16),
                   jax.ShapeDtypeStruct((G * tm, A_HEADS * QK_PAD), bf16),
                   jax.ShapeDtypeStruct((G * tm, A_HEADS * QK_PAD), bf16),
                   jax.ShapeDtypeStruct((G * vt_rows, ATT_T), bf16)],
        scratch_shapes=[pltpu.VMEM((SUB + 8, M_WIDTH), f32),
                        pltpu.VMEM((M_HEADS, M_DV + ONES_ROWS, M_DKP), f32),
                        pltpu.VMEM((8, 128), f32),
                        pltpu.VMEM((nch, M_DV + ONES_ROWS, M_DKP + M_CHUNK), bf16),
                        pltpu.VMEM((nch, M_DKP + M_CHUNK, M_CHUNK), bf16),
                        pltpu.VMEM((nch, M_DV + ONES_ROWS, M_CHUNK), f32),
                        pltpu.VMEM((nch, 8, M_CHUNK), f32),
                        pltpu.VMEM((SUB, M_WIDTH), f32),
                        pltpu.VMEM((SUB, M_WIDTH), f32),
                        pltpu.VMEM((T_END, SUB), f32),
                        pltpu.VMEM((M_HEADS * M_DKP, SUB), bf16),
                        pltpu.VMEM((SUB, M_HEADS * M_DKP), bf16),
                        pltpu.VMEM((SUB, A_HEADS * QK_PAD), f32),
                        pltpu.VMEM((SUB, A_HEADS * NOPE_DIM), f32),
                        pltpu.VMEM((SUB, 128), f32)],
        compiler_params=pltpu.CompilerParams(
            dimension_semantics=("arbitrary",), vmem_limit_bytes=VMEM_LIMIT),
        name="front",
    )(x, cos, sin, *weights)


def _attn_kernel(q_ref, k_ref, vt_ref, g_ref, o_ref, acc_ref, *, tq):
    qi = pl.program_id(1)
    ones = jnp.ones((ONES_ROWS, tq), bf16)
    acc_ref[...] = jnp.zeros(acc_ref.shape, f32)

    def block(j, ms, masked):
        r0 = pl.multiple_of(j * tq, tq)
        v0 = j * (A_HEADS * A_DV)
        out = []
        for hh in range(A_HEADS):
            c0 = hh * QK_PAD
            s = _dot_nt(k_ref[pl.ds(r0, tq), c0:c0 + QK_PAD], q_ref[:, c0:c0 + QK_PAD])
            if masked:
                row = lax.broadcasted_iota(jnp.int32, (tq, tq), 0)
                col = lax.broadcasted_iota(jnp.int32, (tq, tq), 1)
                s = jnp.where(row <= col, s, -jnp.inf)
            m_new = jnp.maximum(ms[hh], jnp.max(s, axis=0, keepdims=True))
            alpha = jnp.exp2(ms[hh] - m_new)
            p = jnp.exp2(s - m_new).astype(bf16)
            vte = jnp.concatenate(
                [vt_ref[pl.ds(pl.multiple_of(v0 + hh * A_DV, A_DV), A_DV), :], ones], axis=0)
            acc_ref[hh] = alpha * acc_ref[hh] + _dot(vte, p)
            out.append(m_new)
        return tuple(out)

    ms = tuple(jnp.full((1, tq), -jnp.inf, f32) for _ in range(A_HEADS))
    ms = lax.fori_loop(0, qi, lambda j, c: block(j, c, False), ms)
    block(qi, ms, True)
    for hh in range(A_HEADS):
        a = acc_ref[hh]
        o = a[0:A_DV, :] / a[A_DV:A_DV + 1, :]
        ms2 = jnp.sum(o * o, axis=0, keepdims=True) * (1.0 / A_DV)
        y = o * lax.rsqrt(ms2 + EPS) * g_ref[hh * A_DV:(hh + 1) * A_DV, :]
        o_ref[:, hh * A_DV:(hh + 1) * A_DV] = y.T.astype(bf16)


def _attention(q, k, vt, g, B, S, tq):
    T = B * S
    nq = S // tq
    return pl.pallas_call(
        functools.partial(_attn_kernel, tq=tq),
        grid=(B, nq),
        in_specs=[pl.BlockSpec((tq, A_HEADS * QK_PAD), lambda b, i: (b * nq + i, 0)),
                  pl.BlockSpec((S, A_HEADS * QK_PAD), lambda b, i: (b, 0)),
                  pl.BlockSpec((nq * A_HEADS * A_DV, tq), lambda b, i: (b, 0)),
                  pl.BlockSpec((A_HEADS * A_DV, 1), lambda b, i: (0, 0))],
        out_specs=pl.BlockSpec((tq, A_HEADS * A_DV), lambda b, i: (b * nq + i, 0)),
        out_shape=jax.ShapeDtypeStruct((T, A_HEADS * A_DV), bf16),
        scratch_shapes=[pltpu.VMEM((A_HEADS, A_DV + ONES_ROWS, tq), f32)],
        compiler_params=pltpu.CompilerParams(
            dimension_semantics=("arbitrary", "arbitrary"), vmem_limit_bytes=VMEM_LIMIT),
        name="mla_attention",
    )(q, k, vt, g)


def _memkv_kernel(mem_ref, g_ref, w_ref, kg_ref, k_ref, v_ref):
    mn = _rms(mem_ref[...], g_ref[...]).astype(bf16)
    kv = _dot(mn, w_ref[...])
    for hh in range(X_HEADS):
        sl = slice(hh * X_HD, (hh + 1) * X_HD)
        k_ref[:, sl] = _rms(kv[:, sl], kg_ref[...]).astype(bf16)
    v_ref[...] = kv[:, X_WIDTH:].astype(bf16)


def _memkv(mem2d, g, wkv, kg, B, Nm):
    return pl.pallas_call(
        _memkv_kernel,
        grid=(B,),
        in_specs=[pl.BlockSpec((Nm, D_MODEL), lambda b: (b, 0)),
                  pl.BlockSpec((1, D_MODEL), lambda b: (0, 0)),
                  pl.BlockSpec((D_MODEL, 2 * X_WIDTH), lambda b: (0, 0)),
                  pl.BlockSpec((1, X_HD), lambda b: (0, 0))],
        out_specs=[pl.BlockSpec((Nm, X_WIDTH), lambda b: (b, 0)),
                   pl.BlockSpec((Nm, X_WIDTH), lambda b: (b, 0))],
        out_shape=[jax.ShapeDtypeStruct((B * Nm, X_WIDTH), bf16)] * 2,
        compiler_params=pltpu.CompilerParams(
            dimension_semantics=("arbitrary",), vmem_limit_bytes=VMEM_LIMIT),
        name="mem_kv",
    )(mem2d, g, wkv, kg)


def _back_kernel(x_ref, ym_ref, ya_ref, wout_ref, gx_ref, wqx_ref, xqg_ref, kx_ref, vx_ref,
                 wox_ref, gf_ref, w1_ref, w2_ref, out_ref):
    x1 = (x_ref[...] + _dot(ym_ref[...], wout_ref[0:M_WIDTH, :])
          + _dot(ya_ref[...], wout_ref[M_WIDTH:2 * M_WIDTH, :]))

    hn = _rms(x1, gx_ref[...]).astype(bf16)
    qx = _dot(hn, wqx_ref[...])
    heads = []
    for hh in range(X_HEADS):
        sl = slice(hh * X_HD, (hh + 1) * X_HD)
        qh = (_rms(qx[:, sl], xqg_ref[...]) * (X_HD ** -0.5)).astype(bf16)
        s = _dot_nt(qh, kx_ref[:, sl])
        p = jnp.exp(s - jnp.max(s, axis=-1, keepdims=True))
        o = _dot(p.astype(bf16), vx_ref[:, sl]) / jnp.sum(p, axis=-1, keepdims=True)
        heads.append(o.astype(bf16))
    x2 = x1 + _dot(jnp.concatenate(heads, axis=-1), wox_ref[...])

    hf = _rms(x2, gf_ref[...]).astype(bf16)
    acc = x2
    for c in range(D_FF // FF_CHUNK):
        sl = slice(c * FF_CHUNK, (c + 1) * FF_CHUNK)
        a = jnp.maximum(_dot(hf, w1_ref[:, sl]), 0.0)
        acc = acc + _dot((a * a).astype(bf16), w2_ref[sl, :])
    out_ref[...] = acc


def _back(x, ym, ya, kx, vx, w, B, S, Nm, tm):
    T = B * S
    row = lambda i: (i, 0)
    c2 = lambda i: (0, 0)
    per_b = lambda i: ((i * tm) // S, 0)

    def const(a):
        return pl.BlockSpec(a.shape, c2, pipeline_mode=pl.Buffered(1))

    return pl.pallas_call(
        _back_kernel,
        grid=(T // tm,),
        in_specs=[pl.BlockSpec((tm, D_MODEL), row),
                  pl.BlockSpec((tm, M_WIDTH), row),
                  pl.BlockSpec((tm, M_WIDTH), row),
                  const(w["wout"]), const(w["gx"]), const(w["wqx"]), const(w["xqg"]),
                  pl.BlockSpec((Nm, X_WIDTH), per_b),
                  pl.BlockSpec((Nm, X_WIDTH), per_b),
                  const(w["wox"]), const(w["gf"]), const(w["w1"]), const(w["w2"])],
        out_specs=pl.BlockSpec((tm, D_MODEL), row),
        out_shape=jax.ShapeDtypeStruct((T, D_MODEL), f32),
        compiler_params=pltpu.CompilerParams(
            dimension_semantics=("arbitrary",), vmem_limit_bytes=VMEM_LIMIT),
        name="back",
    )(x, ym, ya, w["wout"], w["gx"], w["wqx"], w["xqg"], kx, vx, w["wox"], w["gf"], w["w1"], w["w2"])


def _rope_pad(a):
    z = jnp.zeros(a.shape[:-1] + (ROPE_DIM // 2,), a.dtype)
    return jnp.concatenate([a[..., :32], z, a[..., 32:], z], axis=-1)


def _block_diag_heads(w):
    H = w.shape[0]
    wp = jnp.pad(w, ((0, 0), (0, 0), (0, M_DKP - M_DK)))
    eye = jnp.eye(H, dtype=w.dtype)
    return (wp[:, :, None, :] * eye[:, None, :, None]).reshape(H * M_DV, H * M_DKP)


def _layer_weights(l, norm_mix_g, w_in, conv_w, conv_b, wq_m, wk_m, b_igate, b_fgate, m_out_g,
                   cq_norm_g, ckv_norm_g, w_uq, w_ukv, qk_norm_q, qk_norm_k, a_out_g, w_out,
                   norm_x_g, norm_mem_g, wq_x, wkv_x, xq_norm_g, xk_norm_g, wo_x,
                   norm_ffn_g, w_ff1, w_ff2):
    wi = w_in[l].astype(bf16)
    o_v, o_o, o_g = M_WIDTH, 2 * M_WIDTH, 3 * M_WIDTH
    o_f = o_g + M_HEADS
    o_cq = o_f + M_HEADS
    o_ckv = o_cq + Q_LORA
    o_kr = o_ckv + KV_LORA
    wnat = jnp.concatenate([wi[:, :o_v], wi[:, o_o:o_g], wi[:, o_cq:o_kr], _rope_pad(wi[:, o_kr:])], axis=1)
    zg = jnp.zeros((8 - M_HEADS, D_MODEL), wi.dtype)
    wtr = jnp.concatenate([wi[:, o_v:o_o].T, wi[:, o_g:o_f].T, zg, wi[:, o_f:o_cq].T, zg], axis=0)
    z4 = jnp.zeros((8 - M_HEADS,), f32)
    gbias = jnp.concatenate([b_igate[l], z4, b_fgate[l], z4])[:, None]
    wuq = w_uq[l].astype(bf16).reshape(Q_LORA, A_HEADS, A_QK)
    wuq = jnp.concatenate([wuq[..., :NOPE_DIM], _rope_pad(wuq[..., NOPE_DIM:])], axis=-1)
    wukv = w_ukv[l].astype(bf16).reshape(KV_LORA, A_HEADS, NOPE_DIM + A_DV)
    return {
        "gmix": norm_mix_g[l][None, :],
        "wnat": wnat,
        "wtr": wtr,
        "convw": conv_w[l],
        "convb": conv_b[l][None, :],
        "wqt": _block_diag_heads((wq_m[l] * (M_DK ** -0.5)).astype(bf16)).T,
        "wk": _block_diag_heads(wk_m[l].astype(bf16)),
        "gbias": gbias,
        "mog": m_out_g[l].reshape(1, M_WIDTH),
        "cqg": cq_norm_g[l][None, :],
        "ckvg": ckv_norm_g[l][None, :],
        "wuq": wuq.reshape(Q_LORA, A_HEADS * QK_PAD),
        "wukvk": wukv[..., :NOPE_DIM].reshape(KV_LORA, A_HEADS * NOPE_DIM),
        "wukvvt": wukv[..., NOPE_DIM:].reshape(KV_LORA, A_HEADS * A_DV).T,
        "gq": jnp.concatenate([qk_norm_q[l][:NOPE_DIM], _rope_pad(qk_norm_q[l][NOPE_DIM:])])[None, :],
        "gk": jnp.concatenate([qk_norm_k[l][:NOPE_DIM], _rope_pad(qk_norm_k[l][NOPE_DIM:])])[None, :],
        "aog": a_out_g[l].reshape(A_HEADS * A_DV, 1),
        "wout": w_out[l].astype(bf16),
        "gx": norm_x_g[l][None, :],
        "gmem": norm_mem_g[l][None, :],
        "wqx": wq_x[l].astype(bf16),
        "wkvx": wkv_x[l].astype(bf16),
        "xqg": xq_norm_g[l][None, :],
        "xkg": xk_norm_g[l][None, :],
        "wox": wo_x[l].astype(bf16),
        "gf": norm_ffn_g[l][None, :],
        "w1": w_ff1[l].astype(bf16),
        "w2": w_ff2[l].astype(bf16),
    }


def kernel(x, mem, positions, norm_mix_g, w_in, conv_w, conv_b, wq_m, wk_m, b_igate, b_fgate, m_out_g, cq_norm_g, ckv_norm_g, w_uq, w_ukv, qk_norm_q, qk_norm_k, a_out_g, w_out, norm_x_g, norm_mem_g, wq_x, wkv_x, xq_norm_g, xk_norm_g, wo_x, norm_ffn_g, w_ff1, w_ff2):
    B, S, D = x.shape
    Nm = mem.shape[1]
    depth = w_in.shape[0]
    assert D == D_MODEL and S % (2 * SUB) == 0
    tm_back = SUB
    params = (norm_mix_g, w_in, conv_w, conv_b, wq_m, wk_m, b_igate, b_fgate, m_out_g,
              cq_norm_g, ckv_norm_g, w_uq, w_ukv, qk_norm_q, qk_norm_k, a_out_g, w_out,
              norm_x_g, norm_mem_g, wq_x, wkv_x, xq_norm_g, xk_norm_g, wo_x,
              norm_ffn_g, w_ff1, w_ff2)

    cos, sin = _rope_tables(positions)
    xt = x.reshape(B * S, D)
    mem2d = mem.reshape(B * Nm, D)
    for l in range(depth):
        w = _layer_weights(l, *params)
        ym, q, k, vt = _front(xt, cos, sin, w, B, S)
        ya = _attention(q, k, vt, w["aog"], B, S, ATT_T)
        kx, vx = _memkv(mem2d, w["gmem"], w["wkvx"], w["xkg"], B, Nm)
        xt = _back(xt, ym, ya, kx, vx, w, B, S, Nm, tm_back)
    return xt.reshape(B, S, D)
```

```python
import functools

import jax
import jax.numpy as jnp
from jax import lax
from jax.experimental import pallas as pl
from jax.experimental.pallas import tpu as pltpu

f32 = jnp.float32
bf16 = jnp.bfloat16

EPS = 1e-6
D_MODEL = 1024
M_HEADS = 4
M_WIDTH = 512
M_DV = 128
M_DK = 64
CONV_K = 4
A_HEADS = 4
A_DV = 128
Q_LORA = 256
KV_LORA = 128
NOPE_DIM = 128
ROPE_DIM = 64
A_QK = NOPE_DIM + ROPE_DIM
ROPE_THETA = 10000.0
X_HEADS = 4
X_HD = 128
X_WIDTH = 512
D_FF = 4096
FF_CHUNK = 1024

M_CHUNK = 128
M_DKP = 128
SUB = 512
ATT_T = 256
QK_PAD = 256
ONES_ROWS = 16
N_U, N_O, N_CQ, N_CKV, N_KR, N_END = 0, 512, 1024, 1280, 1408, 1536
T_V, T_GI, T_GF, T_END = 0, 512, 520, 528

VMEM_LIMIT = 56 * 1024 * 1024
LOG2E = 1.4426950408889634


def _dot(a, b):
    return jnp.dot(a, b, preferred_element_type=f32)


def _dot_nt(a, b):
    return lax.dot_general(a, b, (((1,), (1,)), ((), ())), preferred_element_type=f32)


def _rms(x, g, n=None):
    n = x.shape[-1] if n is None else n
    ms = jnp.sum(x * x, axis=-1, keepdims=True) * (1.0 / n)
    return x * lax.rsqrt(ms + EPS) * g


def _sigmoid(x):
    return 1.0 / (1.0 + jnp.exp(-x))


def _log_sigmoid(x):
    return jnp.minimum(x, 0.0) - jnp.log1p(jnp.exp(-jnp.abs(x)))


def _split3(x):
    a = x.astype(bf16)
    r = x - a.astype(f32)
    b = r.astype(bf16)
    c = (r - b.astype(f32)).astype(bf16)
    return a, b, c


def _rope_kernel(pos_ref, inv_ref, sgn_ref, cos_ref, sin_ref):
    ang = pos_ref[...].astype(f32) * inv_ref[...]
    cos_ref[...] = jnp.cos(ang) * jnp.abs(sgn_ref[...])
    sin_ref[...] = jnp.sin(ang) * sgn_ref[...]


def _rope_tables(positions):
    T = positions.size
    tm = 2 * SUB
    assert T % tm == 0
    inv = 1.0 / (ROPE_THETA ** (jnp.arange(0, ROPE_DIM, 2, dtype=f32) / ROPE_DIM))
    z = jnp.zeros((ROPE_DIM // 2,), f32)
    o = jnp.ones((ROPE_DIM // 2,), f32)
    inv_p = jnp.concatenate([inv, z, inv, z])[None, :]
    sgn = jnp.concatenate([-o, z, o, z])[None, :]
    return pl.pallas_call(
        _rope_kernel,
        grid=(T // tm,),
        in_specs=[pl.BlockSpec((tm, 1), lambda i: (i, 0)),
                  pl.BlockSpec((1, 128), lambda i: (0, 0)),
                  pl.BlockSpec((1, 128), lambda i: (0, 0))],
        out_specs=[pl.BlockSpec((tm, 128), lambda i: (i, 0)),
                   pl.BlockSpec((tm, 128), lambda i: (i, 0))],
        out_shape=[jax.ShapeDtypeStruct((T, 128), f32)] * 2,
        name="rope_tables",
    )(positions.reshape(T, 1), inv_p, sgn)


def _front_kernel(x_ref, cos_ref, sin_ref, gmix_ref, wnat_ref, wtr_ref, convw_ref, convb_ref,
                  wqt_ref, wk_ref, gbias_ref, mog_ref, cqg_ref, ckvg_ref,
                  wuq_ref, wukvk_ref, wukvvt_ref, gq_ref, gk_ref,
                  ym_ref, q_ref, k_ref, vt_ref,
                  ubuf_ref, st_ref, mst_ref, lhs_ref, rhs_ref, num_ref, mj_ref, gate_ref,
                  sog_ref, spt_ref, sqt_ref, sk_ref, sqa_ref, skn_ref, skr_ref, *, nt):
    L = M_CHUNK
    NC = SUB // L
    g = pl.program_id(0)

    @pl.when(g == 0)
    def _():
        for ref in (ubuf_ref, st_ref, mst_ref, num_ref, mj_ref, gate_ref,
                    sog_ref, spt_ref, sqt_ref, sk_ref, sqa_ref, skn_ref, skr_ref):
            ref[...] = jnp.zeros(ref.shape, ref.dtype)

    keep_conv = jnp.where(g % nt != 0, 1.0, 0.0).astype(f32)
    keep_state = jnp.where((g + nt - 1) % nt != 0, 1.0, 0.0).astype(f32)

    srow = lax.broadcasted_iota(jnp.int32, (L, L), 0)
    qcol = lax.broadcasted_iota(jnp.int32, (L, L), 1)
    causal_t = srow <= qcol
    triu = jnp.where(causal_t, 1.0, 0.0).astype(bf16)
    ones_f = jnp.ones((ONES_ROWS, L), f32)
    ones_b = jnp.ones((ONES_ROWS, L), bf16)
    zpad = jnp.zeros((L - 8, L), f32)
    items = [(c, hh) for c in range(NC) for hh in range(M_HEADS)]

    h = _rms(x_ref[...], gmix_ref[...]).astype(bf16)
    pn = _dot(h, wnat_ref[...])
    pt = _dot_nt(wtr_ref[...], h)

    def slices(c, hh):
        cols = slice(c * L, (c + 1) * L)
        return (sk_ref[c * L:(c + 1) * L, hh * M_DKP:(hh + 1) * M_DKP],
                sqt_ref[hh * M_DKP:(hh + 1) * M_DKP, cols],
                spt_ref[T_V + hh * M_DV:T_V + (hh + 1) * M_DV, cols])

    scores = [_dot(*slices(c, hh)[0:2]) for c, hh in items]
    gi = spt_ref[T_GI:T_GI + 8, :] + gbias_ref[0:8, :]
    logf = _log_sigmoid(spt_ref[T_GF:T_GF + 8, :] + gbias_ref[8:16, :])
    m_run = mst_ref[:, 0:1] * keep_state
    gts = []
    for c in range(NC):
        cols = slice(c * L, (c + 1) * L)
        bsum = _dot(jnp.concatenate(_split3(logf[:, cols]), axis=0), triu)
        b = bsum[0:8] + bsum[8:16] + bsum[16:24]
        ic = gi[:, cols]
        btot = b[:, L - 1:L]
        a = btot - b + ic
        mloc = jnp.max(a, axis=1, keepdims=True)
        wa = jnp.exp(a - mloc)
        m_prev = m_run
        m_run = jnp.maximum(btot + m_prev, mloc)
        s_old = jnp.exp(btot + m_prev - m_run)
        s_loc = jnp.exp(mloc - m_run)
        r_t = jnp.concatenate([ic - b, zpad], axis=0).T
        gts.append((b, wa, s_old, s_loc, r_t, b + m_prev))
    mst_ref[...] = jnp.broadcast_to(m_run, (8, 128))

    cqn = _rms(pn[:, N_CQ:N_CQ + Q_LORA], cqg_ref[...]).astype(bf16)
    qa = _dot(cqn, wuq_ref[...])
    ckvn = _rms(pn[:, N_CKV:N_CKV + KV_LORA], ckvg_ref[...]).astype(bf16)
    kn = _dot(ckvn, wukvk_ref[...])
    vt = _dot_nt(wukvvt_ref[...], ckvn).astype(bf16)
    for c in range(SUB // ATT_T):
        vt_ref[c * A_HEADS * A_DV:(c + 1) * A_HEADS * A_DV, :] = vt[:, c * ATT_T:(c + 1) * ATT_T]
    u = pn[:, N_U:N_U + M_WIDTH]
    ubuf_ref[0:8, :] = ubuf_ref[0:8, :] * keep_conv
    ubuf_ref[8:8 + SUB, :] = u
    acc = convb_ref[...] + convw_ref[CONV_K - 1:CONV_K, :] * u
    for j in range(CONV_K - 1):
        off = 8 - (CONV_K - 1) + j
        acc = acc + convw_ref[j:j + 1, :] * ubuf_ref[off:off + SUB, :]
    ubuf_ref[0:8, :] = u[SUB - 8:SUB, :]
    ucb = (acc * _sigmoid(acc)).astype(bf16)
    qt_all = _dot_nt(wqt_ref[...], ucb).astype(bf16)
    k_all = _dot(ucb, wk_ref[...]).astype(bf16)

    for i, (c, hh) in enumerate(items):
        hs = slice(hh * M_DV, (hh + 1) * M_DV)
        num = num_ref[i]
        den = num[M_DV:M_DV + 1, :]
        hm = num[0:M_DV, :] / jnp.maximum(jnp.abs(den), jnp.exp(-mj_ref[i][0:1, :]))
        hn = hm * lax.rsqrt(jnp.sum(hm * hm, axis=0, keepdims=True) * (1.0 / M_DV) + EPS)
        ym_ref[c * L:(c + 1) * L, hs] = (hn.T * gate_ref[c * L:(c + 1) * L, hs]).astype(bf16)

    states = [st_ref[hh] * keep_state for hh in range(M_HEADS)]
    for i, (c, hh) in enumerate(items):
        b, wa, s_old, s_loc, r_t, inter_all = gts[c]
        k_c, qt_c, vt_c = slices(c, hh)
        st = states[hh]
        vt_w = jnp.concatenate([vt_c, ones_f], axis=0) * wa[hh:hh + 1, :]
        loc = _dot(vt_w.astype(bf16), k_c)
        states[hh] = s_old[hh:hh + 1, :] * st + s_loc[hh:hh + 1, :] * loc
        inter = inter_all[hh:hh + 1, :]
        dlog = jnp.where(causal_t, r_t[:, hh:hh + 1] + b[hh:hh + 1, :], -jnp.inf)
        mj = jnp.maximum(inter, jnp.max(dlog, axis=0, keepdims=True))
        qk = (scores[i] * jnp.exp(dlog - mj)).astype(bf16)
        s_int = jnp.exp(inter - mj)
        rhs_ref[i] = jnp.concatenate([(qt_c.astype(f32) * s_int).astype(bf16), qk], axis=0)
        lhs_ref[i] = jnp.concatenate(
            [st.astype(bf16), jnp.concatenate([vt_c.astype(bf16), ones_b], axis=0)], axis=1)
        mj_ref[i] = jnp.broadcast_to(mj, (8, L))
    for hh in range(M_HEADS):
        st_ref[hh] = states[hh]
    for i in range(len(items)):
        num_ref[i] = _dot(lhs_ref[i], rhs_ref[i])
    gate_ref[...] = mog_ref[...] * _sigmoid(sog_ref[...])

    cos = cos_ref[...]
    sin = sin_ref[...]

    def rope(t):
        return t * cos + pltpu.roll(t, 64, 1) * sin

    qscale = A_QK ** -0.5 * LOG2E
    gq_n, gq_r = gq_ref[:, 0:128] * qscale, gq_ref[:, 128:256] * qscale
    gk_n, gk_r = gk_ref[:, 0:128], gk_ref[:, 128:256]
    kr = rope(_rms(skr_ref[...], gk_r, ROPE_DIM)).astype(bf16)
    for hh in range(A_HEADS):
        c0 = hh * QK_PAD
        qn = _rms(sqa_ref[:, c0:c0 + 128], gq_n)
        qr = rope(_rms(sqa_ref[:, c0 + 128:c0 + 256], gq_r, ROPE_DIM))
        q_ref[:, c0:c0 + 128] = qn.astype(bf16)
        q_ref[:, c0 + 128:c0 + 256] = qr.astype(bf16)
        k_ref[:, c0:c0 + 128] = _rms(skn_ref[:, hh * 128:(hh + 1) * 128], gk_n).astype(bf16)
        k_ref[:, c0 + 128:c0 + 256] = kr

    sog_ref[...] = pn[:, N_O:N_O + M_WIDTH]
    spt_ref[...] = pt
    sqt_ref[...] = qt_all
    sk_ref[...] = k_all
    sqa_ref[...] = qa
    skn_ref[...] = kn
    skr_ref[...] = pn[:, N_KR:N_KR + 128]


def _front(x, cos, sin, w, B, S):
    tm = SUB
    nt = S // tm
    G = B * nt
    nch = (SUB // M_CHUNK) * M_HEADS
    vt_rows = (tm // ATT_T) * A_HEADS * A_DV

    def lag(d):
        return lambda g: (jnp.clip(g - d, 0, G - 1), 0)

    c2 = lambda g: (0, 0)

    def full(a):
        return pl.BlockSpec(a.shape, c2)

    weights = [w["gmix"], w["wnat"], w["wtr"], w["convw"], w["convb"], w["wqt"], w["wk"], w["gbias"],
               w["mog"], w["cqg"], w["ckvg"], w["wuq"], w["wukvk"], w["wukvvt"], w["gq"], w["gk"]]
    return pl.pallas_call(
        functools.partial(_front_kernel, nt=nt),
        grid=(G + 2,),
        in_specs=[pl.BlockSpec((tm, D_MODEL), lag(0)),
                  pl.BlockSpec((tm, 128), lag(1)),
                  pl.BlockSpec((tm, 128), lag(1))] + [full(a) for a in weights],
        out_specs=[pl.BlockSpec((tm, M_WIDTH), lag(2)),
                   pl.BlockSpec((tm, A_HEADS * QK_PAD), lag(1)),
                   pl.BlockSpec((tm, A_HEADS * QK_PAD), lag(1)),
                   pl.BlockSpec((vt_rows, ATT_T), lag(0))],
        out_shape=[jax.ShapeDtypeStruct((G * tm, M_WIDTH), bf16),
                   jax.ShapeDtypeStruct((G * tm, A_HEADS * QK_PAD), bf16),
                   jax.ShapeDtypeStruct((G * tm, A_HEADS * QK_PAD), bf16),
                   jax.ShapeDtypeStruct((G * vt_rows, ATT_T), bf16)],
        scratch_shapes=[pltpu.VMEM((SUB + 8, M_WIDTH), f32),
                        pltpu.VMEM((M_HEADS, M_DV + ONES_ROWS, M_DKP), f32),
                        pltpu.VMEM((8, 128), f32),
                        pltpu.VMEM((nch, M_DV + ONES_ROWS, M_DKP + M_CHUNK), bf16),
                        pltpu.VMEM((nch, M_DKP + M_CHUNK, M_CHUNK), bf16),
                        pltpu.VMEM((nch, M_DV + ONES_ROWS, M_CHUNK), f32),
                        pltpu.VMEM((nch, 8, M_CHUNK), f32),
                        pltpu.VMEM((SUB, M_WIDTH), f32),
                        pltpu.VMEM((SUB, M_WIDTH), f32),
                        pltpu.VMEM((T_END, SUB), f32),
                        pltpu.VMEM((M_HEADS * M_DKP, SUB), bf16),
                        pltpu.VMEM((SUB, M_HEADS * M_DKP), bf16),
                        pltpu.VMEM((SUB, A_HEADS * QK_PAD), f32),
                        pltpu.VMEM((SUB, A_HEADS * NOPE_DIM), f32),
                        pltpu.VMEM((SUB, 128), f32)],
        compiler_params=pltpu.CompilerParams(
            dimension_semantics=("arbitrary",), vmem_limit_bytes=VMEM_LIMIT),
        name="front",
    )(x, cos, sin, *weights)


def _attn_kernel(q_ref, k_ref, vt_ref, g_ref, o_ref, acc_ref, *, tq):
    qi = pl.program_id(1)
    ones = jnp.ones((ONES_ROWS, tq), bf16)
    acc_ref[...] = jnp.zeros(acc_ref.shape, f32)

    def block(j, ms, masked):
        r0 = pl.multiple_of(j * tq, tq)
        v0 = j * (A_HEADS * A_DV)
        out = []
        for hh in range(A_HEADS):
            c0 = hh * QK_PAD
            s = _dot_nt(k_ref[pl.ds(r0, tq), c0:c0 + QK_PAD], q_ref[:, c0:c0 + QK_PAD])
            if masked:
                row = lax.broadcasted_iota(jnp.int32, (tq, tq), 0)
                col = lax.broadcasted_iota(jnp.int32, (tq, tq), 1)
                s = jnp.where(row <= col, s, -jnp.inf)
            m_new = jnp.maximum(ms[hh], jnp.max(s, axis=0, keepdims=True))
            alpha = jnp.exp2(ms[hh] - m_new)
            p = jnp.exp2(s - m_new).astype(bf16)
            vte = jnp.concatenate(
                [vt_ref[pl.ds(pl.multiple_of(v0 + hh * A_DV, A_DV), A_DV), :], ones], axis=0)
            acc_ref[hh] = alpha * acc_ref[hh] + _dot(vte, p)
            out.append(m_new)
        return tuple(out)

    ms = tuple(jnp.full((1, tq), -jnp.inf, f32) for _ in range(A_HEADS))
    ms = lax.fori_loop(0, qi, lambda j, c: block(j, c, False), ms)
    block(qi, ms, True)
    for hh in range(A_HEADS):
        a = acc_ref[hh]
        o = a[0:A_DV, :] / a[A_DV:A_DV + 1, :]
        ms2 = jnp.sum(o * o, axis=0, keepdims=True) * (1.0 / A_DV)
        y = o * lax.rsqrt(ms2 + EPS) * g_ref[hh * A_DV:(hh + 1) * A_DV, :]
        o_ref[:, hh * A_DV:(hh + 1) * A_DV] = y.T.astype(bf16)


def _attention(q, k, vt, g, B, S, tq):
    T = B * S
    nq = S // tq
    return pl.pallas_call(
        functools.partial(_attn_kernel, tq=tq),
        grid=(B, nq),
        in_specs=[pl.BlockSpec((tq, A_HEADS * QK_PAD), lambda b, i: (b * nq + i, 0)),
                  pl.BlockSpec((S, A_HEADS * QK_PAD), lambda b, i: (b, 0)),
                  pl.BlockSpec((nq * A_HEADS * A_DV, tq), lambda b, i: (b, 0)),
                  pl.BlockSpec((A_HEADS * A_DV, 1), lambda b, i: (0, 0))],
        out_specs=pl.BlockSpec((tq, A_HEADS * A_DV), lambda b, i: (b * nq + i, 0)),
        out_shape=jax.ShapeDtypeStruct((T, A_HEADS * A_DV), bf16),
        scratch_shapes=[pltpu.VMEM((A_HEADS, A_DV + ONES_ROWS, tq), f32)],
        compiler_params=pltpu.CompilerParams(
            dimension_semantics=("arbitrary", "arbitrary"), vmem_limit_bytes=VMEM_LIMIT),
        name="mla_attention",
    )(q, k, vt, g)


def _memkv_kernel(mem_ref, g_ref, w_ref, kg_ref, k_ref, v_ref):
    mn = _rms(mem_ref[...], g_ref[...]).astype(bf16)
    kv = _dot(mn, w_ref[...])
    for hh in range(X_HEADS):
        sl = slice(hh * X_HD, (hh + 1) * X_HD)
        k_ref[:, sl] = _rms(kv[:, sl], kg_ref[...]).astype(bf16)
    v_ref[...] = kv[:, X_WIDTH:].astype(bf16)


def _memkv(mem2d, g, wkv, kg, B, Nm):
    return pl.pallas_call(
        _memkv_kernel,
        grid=(B,),
        in_specs=[pl.BlockSpec((Nm, D_MODEL), lambda b: (b, 0)),
                  pl.BlockSpec((1, D_MODEL), lambda b: (0, 0)),
                  pl.BlockSpec((D_MODEL, 2 * X_WIDTH), lambda b: (0, 0)),
                  pl.BlockSpec((1, X_HD), lambda b: (0, 0))],
        out_specs=[pl.BlockSpec((Nm, X_WIDTH), lambda b: (b, 0)),
                   pl.BlockSpec((Nm, X_WIDTH), lambda b: (b, 0))],
        out_shape=[jax.ShapeDtypeStruct((B * Nm, X_WIDTH), bf16)] * 2,
        compiler_params=pltpu.CompilerParams(
            dimension_semantics=("arbitrary",), vmem_limit_bytes=VMEM_LIMIT),
        name="mem_kv",
    )(mem2d, g, wkv, kg)


def _back_kernel(x_ref, ym_ref, ya_ref, wout_ref, gx_ref, wqx_ref, xqg_ref, kx_ref, vx_ref,
                 wox_ref, gf_ref, w1_ref, w2_ref, out_ref, x2_ref):
    @pl.when(pl.program_id(0) == 0)
    def _():
        x2_ref[...] = jnp.zeros(x2_ref.shape, f32)

    x2p = x2_ref[...]
    hf = _rms(x2p, gf_ref[...]).astype(bf16)
    mlp = {"acc": x2p, "a": None}

    def mlp_step(k):
        c, second = divmod(k, 2)
        sl = slice(c * FF_CHUNK, (c + 1) * FF_CHUNK)
        if not second:
            a = jnp.maximum(_dot(hf, w1_ref[:, sl]), 0.0)
            mlp["a"] = (a * a).astype(bf16)
        else:
            mlp["acc"] = mlp["acc"] + _dot(mlp["a"], w2_ref[sl, :])

    nsteps = 2 * (D_FF // FF_CHUNK)
    x1 = (x_ref[...] + _dot(ym_ref[...], wout_ref[0:M_WIDTH, :])
          + _dot(ya_ref[...], wout_ref[M_WIDTH:2 * M_WIDTH, :]))
    mlp_step(0)
    hn = _rms(x1, gx_ref[...]).astype(bf16)
    qx = _dot(hn, wqx_ref[...])
    mlp_step(1)
    ps = []
    for hh in range(X_HEADS):
        sl = slice(hh * X_HD, (hh + 1) * X_HD)
        qh = (_rms(qx[:, sl], xqg_ref[...]) * (X_HD ** -0.5)).astype(bf16)
        s = _dot_nt(qh, kx_ref[:, sl])
        ps.append(jnp.exp(s - jnp.max(s, axis=-1, keepdims=True)))
    mlp_step(2)
    heads = []
    for hh in range(X_HEADS):
        sl = slice(hh * X_HD, (hh + 1) * X_HD)
        o = _dot(ps[hh].astype(bf16), vx_ref[:, sl]) / jnp.sum(ps[hh], axis=-1, keepdims=True)
        heads.append(o.astype(bf16))
    mlp_step(3)
    x2 = x1 + _dot(jnp.concatenate(heads, axis=-1), wox_ref[...])
    for k in range(4, nsteps):
        mlp_step(k)
    out_ref[...] = mlp["acc"]
    x2_ref[...] = x2


def _back(x, ym, ya, kx, vx, w, B, S, Nm, tm):
    T = B * S
    G = T // tm
    row = lambda i: (jnp.minimum(i, G - 1), 0)
    lag = lambda i: (jnp.maximum(i - 1, 0), 0)
    c2 = lambda i: (0, 0)
    per_b = lambda i: ((jnp.minimum(i, G - 1) * tm) // S, 0)

    def const(a):
        return pl.BlockSpec(a.shape, c2, pipeline_mode=pl.Buffered(1))

    return pl.pallas_call(
        _back_kernel,
        grid=(G + 1,),
        in_specs=[pl.BlockSpec((tm, D_MODEL), row),
                  pl.BlockSpec((tm, M_WIDTH), row),
                  pl.BlockSpec((tm, M_WIDTH), row),
                  const(w["wout"]), const(w["gx"]), const(w["wqx"]), const(w["xqg"]),
                  pl.BlockSpec((Nm, X_WIDTH), per_b),
                  pl.BlockSpec((Nm, X_WIDTH), per_b),
                  const(w["wox"]), const(w["gf"]), const(w["w1"]), const(w["w2"])],
        out_specs=pl.BlockSpec((tm, D_MODEL), lag),
        out_shape=jax.ShapeDtypeStruct((T, D_MODEL), f32),
        scratch_shapes=[pltpu.VMEM((tm, D_MODEL), f32)],
        compiler_params=pltpu.CompilerParams(
            dimension_semantics=("arbitrary",), vmem_limit_bytes=VMEM_LIMIT),
        name="back",
    )(x, ym, ya, w["wout"], w["gx"], w["wqx"], w["xqg"], kx, vx, w["wox"], w["gf"], w["w1"], w["w2"])


def _rope_pad(a):
    z = jnp.zeros(a.shape[:-1] + (ROPE_DIM // 2,), a.dtype)
    return jnp.concatenate([a[..., :32], z, a[..., 32:], z], axis=-1)


def _block_diag_heads(w):
    H = w.shape[0]
    wp = jnp.pad(w, ((0, 0), (0, 0), (0, M_DKP - M_DK)))
    eye = jnp.eye(H, dtype=w.dtype)
    return (wp[:, :, None, :] * eye[:, None, :, None]).reshape(H * M_DV, H * M_DKP)


def _layer_weights(l, norm_mix_g, w_in, conv_w, conv_b, wq_m, wk_m, b_igate, b_fgate, m_out_g,
                   cq_norm_g, ckv_norm_g, w_uq, w_ukv, qk_norm_q, qk_norm_k, a_out_g, w_out,
                   norm_x_g, norm_mem_g, wq_x, wkv_x, xq_norm_g, xk_norm_g, wo_x,
                   norm_ffn_g, w_ff1, w_ff2):
    wi = w_in[l].astype(bf16)
    o_v, o_o, o_g = M_WIDTH, 2 * M_WIDTH, 3 * M_WIDTH
    o_f = o_g + M_HEADS
    o_cq = o_f + M_HEADS
    o_ckv = o_cq + Q_LORA
    o_kr = o_ckv + KV_LORA
    wnat = jnp.concatenate([wi[:, :o_v], wi[:, o_o:o_g], wi[:, o_cq:o_kr], _rope_pad(wi[:, o_kr:])], axis=1)
    zg = jnp.zeros((8 - M_HEADS, D_MODEL), wi.dtype)
    wtr = jnp.concatenate([wi[:, o_v:o_o].T, wi[:, o_g:o_f].T, zg, wi[:, o_f:o_cq].T, zg], axis=0)
    z4 = jnp.zeros((8 - M_HEADS,), f32)
    gbias = jnp.concatenate([b_igate[l], z4, b_fgate[l], z4])[:, None]
    wuq = w_uq[l].astype(bf16).reshape(Q_LORA, A_HEADS, A_QK)
    wuq = jnp.concatenate([wuq[..., :NOPE_DIM], _rope_pad(wuq[..., NOPE_DIM:])], axis=-1)
    wukv = w_ukv[l].astype(bf16).reshape(KV_LORA, A_HEADS, NOPE_DIM + A_DV)
    return {
        "gmix": norm_mix_g[l][None, :],
        "wnat": wnat,
        "wtr": wtr,
        "convw": conv_w[l],
        "convb": conv_b[l][None, :],
        "wqt": _block_diag_heads((wq_m[l] * (M_DK ** -0.5)).astype(bf16)).T,
        "wk": _block_diag_heads(wk_m[l].astype(bf16)),
        "gbias": gbias,
        "mog": m_out_g[l].reshape(1, M_WIDTH),
        "cqg": cq_norm_g[l][None, :],
        "ckvg": ckv_norm_g[l][None, :],
        "wuq": wuq.reshape(Q_LORA, A_HEADS * QK_PAD),
        "wukvk": wukv[..., :NOPE_DIM].reshape(KV_LORA, A_HEADS * NOPE_DIM),
        "wukvvt": wukv[..., NOPE_DIM:].reshape(KV_LORA, A_HEADS * A_DV).T,
        "gq": jnp.concatenate([qk_norm_q[l][:NOPE_DIM], _rope_pad(qk_norm_q[l][NOPE_DIM:])])[None, :],
        "gk": jnp.concatenate([qk_norm_k[l][:NOPE_DIM], _rope_pad(qk_norm_k[l][NOPE_DIM:])])[None, :],
        "aog": a_out_g[l].reshape(A_HEADS * A_DV, 1),
        "wout": w_out[l].astype(bf16),
        "gx": norm_x_g[l][None, :],
        "gmem": norm_mem_g[l][None, :],
        "wqx": wq_x[l].astype(bf16),
        "wkvx": wkv_x[l].astype(bf16),
        "xqg": xq_norm_g[l][None, :],
        "xkg": xk_norm_g[l][None, :],
        "wox": wo_x[l].astype(bf16),
        "gf": norm_ffn_g[l][None, :],
        "w1": w_ff1[l].astype(bf16),
        "w2": w_ff2[l].astype(bf16),
    }


def kernel(x, mem, positions, norm_mix_g, w_in, conv_w, conv_b, wq_m, wk_m, b_igate, b_fgate, m_out_g, cq_norm_g, ckv_norm_g, w_uq, w_ukv, qk_norm_q, qk_norm_k, a_out_g, w_out, norm_x_g, norm_mem_g, wq_x, wkv_x, xq_norm_g, xk_norm_g, wo_x, norm_ffn_g, w_ff1, w_ff2):
    B, S, D = x.shape
    Nm = mem.shape[1]
    depth = w_in.shape[0]
    assert D == D_MODEL and S % (2 * SUB) == 0
    tm_back = SUB
    params = (norm_mix_g, w_in, conv_w, conv_b, wq_m, wk_m, b_igate, b_fgate, m_out_g,
              cq_norm_g, ckv_norm_g, w_uq, w_ukv, qk_norm_q, qk_norm_k, a_out_g, w_out,
              norm_x_g, norm_mem_g, wq_x, wkv_x, xq_norm_g, xk_norm_g, wo_x,
              norm_ffn_g, w_ff1, w_ff2)

    cos, sin = _rope_tables(positions)
    xt = x.reshape(B * S, D)
    mem2d = mem.reshape(B * Nm, D)
    for l in range(depth):
        w = _layer_weights(l, *params)
        ym, q, k, vt = _front(xt, cos, sin, w, B, S)
        ya = _attention(q, k, vt, w["aog"], B, S, ATT_T)
        kx, vx = _memkv(mem2d, w["gmem"], w["wkvx"], w["xkg"], B, Nm)
        xt = _back(xt, ym, ya, kx, vx, w, B, S, Nm, tm_back)
    return xt.reshape(B, S, D)
```

```python
import functools

import jax
import jax.numpy as jnp
from jax import lax
from jax.experimental import pallas as pl
from jax.experimental.pallas import tpu as pltpu

f32 = jnp.float32
bf16 = jnp.bfloat16

EPS = 1e-6
D_MODEL = 1024
M_HEADS = 4
M_WIDTH = 512
M_DV = 128
M_DK = 64
CONV_K = 4
A_HEADS = 4
A_DV = 128
Q_LORA = 256
KV_LORA = 128
NOPE_DIM = 128
ROPE_DIM = 64
A_QK = NOPE_DIM + ROPE_DIM
ROPE_THETA = 10000.0
X_HEADS = 4
X_HD = 128
X_WIDTH = 512
D_FF = 4096
FF_CHUNK = 1024

M_CHUNK = 128
M_DKP = 128
SUB = 512
ATT_T = 256
QK_PAD = 256
ONES_ROWS = 16
N_U, N_O, N_CQ, N_CKV, N_KR, N_END = 0, 512, 1024, 1280, 1408, 1536
T_V, T_GI, T_GF, T_END = 0, 512, 520, 528

VMEM_LIMIT = 56 * 1024 * 1024
LOG2E = 1.4426950408889634


def _dot(a, b):
    return jnp.dot(a, b, preferred_element_type=f32)


def _dot_nt(a, b):
    return lax.dot_general(a, b, (((1,), (1,)), ((), ())), preferred_element_type=f32)


def _rms(x, g, n=None):
    n = x.shape[-1] if n is None else n
    ms = jnp.sum(x * x, axis=-1, keepdims=True) * (1.0 / n)
    return x * lax.rsqrt(ms + EPS) * g


def _sigmoid(x):
    return 1.0 / (1.0 + jnp.exp(-x))


def _log_sigmoid(x):
    return jnp.minimum(x, 0.0) - jnp.log1p(jnp.exp(-jnp.abs(x)))


def _split3(x):
    a = x.astype(bf16)
    r = x - a.astype(f32)
    b = r.astype(bf16)
    c = (r - b.astype(f32)).astype(bf16)
    return a, b, c


def _rope_kernel(pos_ref, inv_ref, sgn_ref, cos_ref, sin_ref):
    ang = pos_ref[...].astype(f32) * inv_ref[...]
    cos_ref[...] = jnp.cos(ang) * jnp.abs(sgn_ref[...])
    sin_ref[...] = jnp.sin(ang) * sgn_ref[...]


def _rope_tables(positions):
    T = positions.size
    tm = 2 * SUB
    assert T % tm == 0
    inv = 1.0 / (ROPE_THETA ** (jnp.arange(0, ROPE_DIM, 2, dtype=f32) / ROPE_DIM))
    z = jnp.zeros((ROPE_DIM // 2,), f32)
    o = jnp.ones((ROPE_DIM // 2,), f32)
    inv_p = jnp.concatenate([inv, z, inv, z])[None, :]
    sgn = jnp.concatenate([-o, z, o, z])[None, :]
    return pl.pallas_call(
        _rope_kernel,
        grid=(T // tm,),
        in_specs=[pl.BlockSpec((tm, 1), lambda i: (i, 0)),
                  pl.BlockSpec((1, 128), lambda i: (0, 0)),
                  pl.BlockSpec((1, 128), lambda i: (0, 0))],
        out_specs=[pl.BlockSpec((tm, 128), lambda i: (i, 0)),
                   pl.BlockSpec((tm, 128), lambda i: (i, 0))],
        out_shape=[jax.ShapeDtypeStruct((T, 128), f32)] * 2,
        name="rope_tables",
    )(positions.reshape(T, 1), inv_p, sgn)


def _front_kernel(x_ref, cos_ref, sin_ref, gmix_ref, wnat_ref, wtr_ref, convw_ref, convb_ref,
                  wqt_ref, wk_ref, gbias_ref, mog_ref, cqg_ref, ckvg_ref,
                  wuq_ref, wukvk_ref, wukvvt_ref, gq_ref, gk_ref,
                  ym_ref, q_ref, k_ref, vt_ref,
                  ubuf_ref, st_ref, mst_ref, lhs_ref, rhs_ref, num_ref, mj_ref, gate_ref,
                  sog_ref, spt_ref, sqt_ref, sk_ref, sqa_ref, skn_ref, skr_ref, *, nt):
    L = M_CHUNK
    NC = SUB // L
    g = pl.program_id(0)

    @pl.when(g == 0)
    def _():
        for ref in (ubuf_ref, st_ref, mst_ref, num_ref, mj_ref, gate_ref,
                    sog_ref, spt_ref, sqt_ref, sk_ref, sqa_ref, skn_ref, skr_ref):
            ref[...] = jnp.zeros(ref.shape, ref.dtype)

    keep_conv = jnp.where(g % nt != 0, 1.0, 0.0).astype(f32)
    keep_state = jnp.where((g + nt - 1) % nt != 0, 1.0, 0.0).astype(f32)

    srow = lax.broadcasted_iota(jnp.int32, (L, L), 0)
    qcol = lax.broadcasted_iota(jnp.int32, (L, L), 1)
    causal_t = srow <= qcol
    triu = jnp.where(causal_t, 1.0, 0.0).astype(bf16)
    ones_f = jnp.ones((ONES_ROWS, L), f32)
    ones_b = jnp.ones((ONES_ROWS, L), bf16)
    zpad = jnp.zeros((L - 8, L), f32)
    items = [(c, hh) for c in range(NC) for hh in range(M_HEADS)]

    h = _rms(x_ref[...], gmix_ref[...]).astype(bf16)
    pn = _dot(h, wnat_ref[...])
    pt = _dot_nt(wtr_ref[...], h)

    def slices(c, hh):
        cols = slice(c * L, (c + 1) * L)
        return (sk_ref[c * L:(c + 1) * L, hh * M_DKP:(hh + 1) * M_DKP],
                sqt_ref[hh * M_DKP:(hh + 1) * M_DKP, cols],
                spt_ref[T_V + hh * M_DV:T_V + (hh + 1) * M_DV, cols])

    scores = [_dot(*slices(c, hh)[0:2]) for c, hh in items]
    gi = spt_ref[T_GI:T_GI + 8, :] + gbias_ref[0:8, :]
    logf = _log_sigmoid(spt_ref[T_GF:T_GF + 8, :] + gbias_ref[8:16, :])
    m_run = mst_ref[:, 0:1] * keep_state
    gts = []
    for c in range(NC):
        cols = slice(c * L, (c + 1) * L)
        bsum = _dot(jnp.concatenate(_split3(logf[:, cols]), axis=0), triu)
        b = bsum[0:8] + bsum[8:16] + bsum[16:24]
        ic = gi[:, cols]
        btot = b[:, L - 1:L]
        a = btot - b + ic
        mloc = jnp.max(a, axis=1, keepdims=True)
        wa = jnp.exp(a - mloc)
        m_prev = m_run
        m_run = jnp.maximum(btot + m_prev, mloc)
        s_old = jnp.exp(btot + m_prev - m_run)
        s_loc = jnp.exp(mloc - m_run)
        r_t = jnp.concatenate([ic - b, zpad], axis=0).T
        gts.append((b, wa, s_old, s_loc, r_t, b + m_prev))
    mst_ref[...] = jnp.broadcast_to(m_run, (8, 128))

    cqn = _rms(pn[:, N_CQ:N_CQ + Q_LORA], cqg_ref[...]).astype(bf16)
    qa = _dot(cqn, wuq_ref[...])
    ckvn = _rms(pn[:, N_CKV:N_CKV + KV_LORA], ckvg_ref[...]).astype(bf16)
    kn = _dot(ckvn, wukvk_ref[...])
    vt = _dot_nt(wukvvt_ref[...], ckvn).astype(bf16)
    for c in range(SUB // ATT_T):
        vt_ref[c * A_HEADS * A_DV:(c + 1) * A_HEADS * A_DV, :] = vt[:, c * ATT_T:(c + 1) * ATT_T]
    u = pn[:, N_U:N_U + M_WIDTH]
    ubuf_ref[0:8, :] = ubuf_ref[0:8, :] * keep_conv
    ubuf_ref[8:8 + SUB, :] = u
    acc = convb_ref[...] + convw_ref[CONV_K - 1:CONV_K, :] * u
    for j in range(CONV_K - 1):
        off = 8 - (CONV_K - 1) + j
        acc = acc + convw_ref[j:j + 1, :] * ubuf_ref[off:off + SUB, :]
    ubuf_ref[0:8, :] = u[SUB - 8:SUB, :]
    ucb = (acc * _sigmoid(acc)).astype(bf16)
    qt_all = _dot_nt(wqt_ref[...], ucb).astype(bf16)
    k_all = _dot(ucb, wk_ref[...]).astype(bf16)

    for i, (c, hh) in enumerate(items):
        hs = slice(hh * M_DV, (hh + 1) * M_DV)
        num = num_ref[i]
        den = num[M_DV:M_DV + 1, :]
        hm = num[0:M_DV, :] / jnp.maximum(jnp.abs(den), jnp.exp(-mj_ref[i][0:1, :]))
        hn = hm * lax.rsqrt(jnp.sum(hm * hm, axis=0, keepdims=True) * (1.0 / M_DV) + EPS)
        ym_ref[c * L:(c + 1) * L, hs] = (hn.T * gate_ref[c * L:(c + 1) * L, hs]).astype(bf16)

    states = [st_ref[hh] * keep_state for hh in range(M_HEADS)]
    for i, (c, hh) in enumerate(items):
        b, wa, s_old, s_loc, r_t, inter_all = gts[c]
        k_c, qt_c, vt_c = slices(c, hh)
        st = states[hh]
        vt_w = jnp.concatenate([vt_c, ones_f], axis=0) * wa[hh:hh + 1, :]
        loc = _dot(vt_w.astype(bf16), k_c)
        states[hh] = s_old[hh:hh + 1, :] * st + s_loc[hh:hh + 1, :] * loc
        inter = inter_all[hh:hh + 1, :]
        dlog = jnp.where(causal_t, r_t[:, hh:hh + 1] + b[hh:hh + 1, :], -jnp.inf)
        mj = jnp.maximum(inter, jnp.max(dlog, axis=0, keepdims=True))
        qk = (scores[i] * jnp.exp(dlog - mj)).astype(bf16)
        s_int = jnp.exp(inter - mj)
        rhs_ref[i] = jnp.concatenate([(qt_c.astype(f32) * s_int).astype(bf16), qk], axis=0)
        lhs_ref[i] = jnp.concatenate(
            [st.astype(bf16), jnp.concatenate([vt_c.astype(bf16), ones_b], axis=0)], axis=1)
        mj_ref[i] = jnp.broadcast_to(mj, (8, L))
    for hh in range(M_HEADS):
        st_ref[hh] = states[hh]
    for i in range(len(items)):
        num_ref[i] = _dot(lhs_ref[i], rhs_ref[i])
    gate_ref[...] = mog_ref[...] * _sigmoid(sog_ref[...])

    cos = cos_ref[...]
    sin = sin_ref[...]

    def rope(t):
        return t * cos + pltpu.roll(t, 64, 1) * sin

    qscale = A_QK ** -0.5 * LOG2E
    gq_n, gq_r = gq_ref[:, 0:128] * qscale, gq_ref[:, 128:256] * qscale
    gk_n, gk_r = gk_ref[:, 0:128], gk_ref[:, 128:256]
    kr = rope(_rms(skr_ref[...], gk_r, ROPE_DIM)).astype(bf16)
    for hh in range(A_HEADS):
        c0 = hh * QK_PAD
        qn = _rms(sqa_ref[:, c0:c0 + 128], gq_n)
        qr = rope(_rms(sqa_ref[:, c0 + 128:c0 + 256], gq_r, ROPE_DIM))
        q_ref[:, c0:c0 + 128] = qn.astype(bf16)
        q_ref[:, c0 + 128:c0 + 256] = qr.astype(bf16)
        k_ref[:, c0:c0 + 128] = _rms(skn_ref[:, hh * 128:(hh + 1) * 128], gk_n).astype(bf16)
        k_ref[:, c0 + 128:c0 + 256] = kr

    sog_ref[...] = pn[:, N_O:N_O + M_WIDTH]
    spt_ref[...] = pt
    sqt_ref[...] = qt_all
    sk_ref[...] = k_all
    sqa_ref[...] = qa
    skn_ref[...] = kn
    skr_ref[...] = pn[:, N_KR:N_KR + 128]


def _front(x, cos, sin, w, B, S):
    tm = SUB
    nt = S // tm
    G = B * nt
    nch = (SUB // M_CHUNK) * M_HEADS
    vt_rows = (tm // ATT_T) * A_HEADS * A_DV

    def lag(d):
        return lambda g: (jnp.clip(g - d, 0, G - 1), 0)

    c2 = lambda g: (0, 0)

    def full(a):
        return pl.BlockSpec(a.shape, c2)

    weights = [w["gmix"], w["wnat"], w["wtr"], w["convw"], w["convb"], w["wqt"], w["wk"], w["gbias"],
               w["mog"], w["cqg"], w["ckvg"], w["wuq"], w["wukvk"], w["wukvvt"], w["gq"], w["gk"]]
    return pl.pallas_call(
        functools.partial(_front_kernel, nt=nt),
        grid=(G + 2,),
        in_specs=[pl.BlockSpec((tm, D_MODEL), lag(0)),
                  pl.BlockSpec((tm, 128), lag(1)),
                  pl.BlockSpec((tm, 128), lag(1))] + [full(a) for a in weights],
        out_specs=[pl.BlockSpec((tm, M_WIDTH), lag(2)),
                   pl.BlockSpec((tm, A_HEADS * QK_PAD), lag(1)),
                   pl.BlockSpec((tm, A_HEADS * QK_PAD), lag(1)),
                   pl.BlockSpec((vt_rows, ATT_T), lag(0))],
        out_shape=[jax.ShapeDtypeStruct((G * tm, M_WIDTH), bf16),
                   jax.ShapeDtypeStruct((G * tm, A_HEADS * QK_PAD), bf16),
                   jax.ShapeDtypeStruct((G * tm, A_HEADS * QK_PAD), bf16),
                   jax.ShapeDtypeStruct((G * vt_rows, ATT_T), bf16)],
        scratch_shapes=[pltpu.VMEM((SUB + 8, M_WIDTH), f32),
                        pltpu.VMEM((M_HEADS, M_DV + ONES_ROWS, M_DKP), f32),
                        pltpu.VMEM((8, 128), f32),
                        pltpu.VMEM((nch, M_DV + ONES_ROWS, M_DKP + M_CHUNK), bf16),
                        pltpu.VMEM((nch, M_DKP + M_CHUNK, M_CHUNK), bf16),
                        pltpu.VMEM((nch, M_DV + ONES_ROWS, M_CHUNK), f32),
                        pltpu.VMEM((nch, 8, M_CHUNK), f32),
                        pltpu.VMEM((SUB, M_WIDTH), f32),
                        pltpu.VMEM((SUB, M_WIDTH), f32),
                        pltpu.VMEM((T_END, SUB), f32),
                        pltpu.VMEM((M_HEADS * M_DKP, SUB), bf16),
                        pltpu.VMEM((SUB, M_HEADS * M_DKP), bf16),
                        pltpu.VMEM((SUB, A_HEADS * QK_PAD), f32),
                        pltpu.VMEM((SUB, A_HEADS * NOPE_DIM), f32),
                        pltpu.VMEM((SUB, 128), f32)],
        compiler_params=pltpu.CompilerParams(
            dimension_semantics=("arbitrary",), vmem_limit_bytes=VMEM_LIMIT),
        name="front",
    )(x, cos, sin, *weights)


def _attn_kernel(q_ref, k_ref, vt_ref, g_ref, o_ref, acc_ref, *, tq):
    qi = pl.program_id(1)
    ones = jnp.ones((ONES_ROWS, tq), bf16)
    acc_ref[...] = jnp.zeros(acc_ref.shape, f32)

    def block(j, ms, masked):
        r0 = pl.multiple_of(j * tq, tq)
        v0 = j * (A_HEADS * A_DV)
        ss = [_dot_nt(k_ref[pl.ds(r0, tq), hh * QK_PAD:(hh + 1) * QK_PAD],
                      q_ref[:, hh * QK_PAD:(hh + 1) * QK_PAD]) for hh in range(A_HEADS)]
        out = []
        for hh in range(A_HEADS):
            s = ss[hh]
            if masked:
                row = lax.broadcasted_iota(jnp.int32, (tq, tq), 0)
                col = lax.broadcasted_iota(jnp.int32, (tq, tq), 1)
                s = jnp.where(row <= col, s, -jnp.inf)
            m_new = jnp.maximum(ms[hh], jnp.max(s, axis=0, keepdims=True))
            alpha = jnp.exp2(ms[hh] - m_new)
            p = jnp.exp2(s - m_new).astype(bf16)
            vte = jnp.concatenate(
                [vt_ref[pl.ds(pl.multiple_of(v0 + hh * A_DV, A_DV), A_DV), :], ones], axis=0)
            acc_ref[hh] = alpha * acc_ref[hh] + _dot(vte, p)
            out.append(m_new)
        return tuple(out)

    ms = tuple(jnp.full((1, tq), -jnp.inf, f32) for _ in range(A_HEADS))
    ms = lax.fori_loop(0, qi, lambda j, c: block(j, c, False), ms)
    block(qi, ms, True)
    for hh in range(A_HEADS):
        a = acc_ref[hh]
        o = a[0:A_DV, :] / a[A_DV:A_DV + 1, :]
        ms2 = jnp.sum(o * o, axis=0, keepdims=True) * (1.0 / A_DV)
        y = o * lax.rsqrt(ms2 + EPS) * g_ref[hh * A_DV:(hh + 1) * A_DV, :]
        o_ref[:, hh * A_DV:(hh + 1) * A_DV] = y.T.astype(bf16)


def _attention(q, k, vt, g, B, S, tq):
    T = B * S
    nq = S // tq
    return pl.pallas_call(
        functools.partial(_attn_kernel, tq=tq),
        grid=(B, nq),
        in_specs=[pl.BlockSpec((tq, A_HEADS * QK_PAD), lambda b, i: (b * nq + i, 0)),
                  pl.BlockSpec((S, A_HEADS * QK_PAD), lambda b, i: (b, 0)),
                  pl.BlockSpec((nq * A_HEADS * A_DV, tq), lambda b, i: (b, 0)),
                  pl.BlockSpec((A_HEADS * A_DV, 1), lambda b, i: (0, 0))],
        out_specs=pl.BlockSpec((tq, A_HEADS * A_DV), lambda b, i: (b * nq + i, 0)),
        out_shape=jax.ShapeDtypeStruct((T, A_HEADS * A_DV), bf16),
        scratch_shapes=[pltpu.VMEM((A_HEADS, A_DV + ONES_ROWS, tq), f32)],
        compiler_params=pltpu.CompilerParams(
            dimension_semantics=("arbitrary", "arbitrary"), vmem_limit_bytes=VMEM_LIMIT),
        name="mla_attention",
    )(q, k, vt, g)


def _memkv_kernel(mem_ref, g_ref, w_ref, kg_ref, k_ref, v_ref):
    mn = _rms(mem_ref[...], g_ref[...]).astype(bf16)
    kv = _dot(mn, w_ref[...])
    for hh in range(X_HEADS):
        sl = slice(hh * X_HD, (hh + 1) * X_HD)
        k_ref[:, sl] = _rms(kv[:, sl], kg_ref[...]).astype(bf16)
    v_ref[...] = kv[:, X_WIDTH:].astype(bf16)


def _memkv(mem2d, g, wkv, kg, B, Nm):
    return pl.pallas_call(
        _memkv_kernel,
        grid=(B,),
        in_specs=[pl.BlockSpec((Nm, D_MODEL), lambda b: (b, 0)),
                  pl.BlockSpec((1, D_MODEL), lambda b: (0, 0)),
                  pl.BlockSpec((D_MODEL, 2 * X_WIDTH), lambda b: (0, 0)),
                  pl.BlockSpec((1, X_HD), lambda b: (0, 0))],
        out_specs=[pl.BlockSpec((Nm, X_WIDTH), lambda b: (b, 0)),
                   pl.BlockSpec((Nm, X_WIDTH), lambda b: (b, 0))],
        out_shape=[jax.ShapeDtypeStruct((B * Nm, X_WIDTH), bf16)] * 2,
        compiler_params=pltpu.CompilerParams(
            dimension_semantics=("arbitrary",), vmem_limit_bytes=VMEM_LIMIT),
        name="mem_kv",
    )(mem2d, g, wkv, kg)


def _back_kernel(x_ref, ym_ref, ya_ref, wout_ref, gx_ref, wqx_ref, xqg_ref, kx_ref, vx_ref,
                 wox_ref, gf_ref, w1_ref, w2_ref, out_ref, x2_ref):
    @pl.when(pl.program_id(0) == 0)
    def _():
        x2_ref[...] = jnp.zeros(x2_ref.shape, f32)

    x2p = x2_ref[...]
    hf = _rms(x2p, gf_ref[...]).astype(bf16)
    mlp = {"acc": x2p, "a": None}

    def mlp_step(k):
        c, second = divmod(k, 2)
        sl = slice(c * FF_CHUNK, (c + 1) * FF_CHUNK)
        if not second:
            a = jnp.maximum(_dot(hf, w1_ref[:, sl]), 0.0)
            mlp["a"] = (a * a).astype(bf16)
        else:
            mlp["acc"] = mlp["acc"] + _dot(mlp["a"], w2_ref[sl, :])

    nsteps = 2 * (D_FF // FF_CHUNK)
    x1 = (x_ref[...] + _dot(ym_ref[...], wout_ref[0:M_WIDTH, :])
          + _dot(ya_ref[...], wout_ref[M_WIDTH:2 * M_WIDTH, :]))
    mlp_step(0)
    hn = _rms(x1, gx_ref[...]).astype(bf16)
    qx = _dot(hn, wqx_ref[...])
    mlp_step(1)
    ps = []
    for hh in range(X_HEADS):
        sl = slice(hh * X_HD, (hh + 1) * X_HD)
        qh = (_rms(qx[:, sl], xqg_ref[...]) * (X_HD ** -0.5)).astype(bf16)
        s = _dot_nt(qh, kx_ref[:, sl])
        ps.append(jnp.exp(s - jnp.max(s, axis=-1, keepdims=True)))
    mlp_step(2)
    heads = []
    for hh in range(X_HEADS):
        sl = slice(hh * X_HD, (hh + 1) * X_HD)
        o = _dot(ps[hh].astype(bf16), vx_ref[:, sl]) / jnp.sum(ps[hh], axis=-1, keepdims=True)
        heads.append(o.astype(bf16))
    mlp_step(3)
    x2 = x1 + _dot(jnp.concatenate(heads, axis=-1), wox_ref[...])
    for k in range(4, nsteps):
        mlp_step(k)
    out_ref[...] = mlp["acc"]
    x2_ref[...] = x2


def _back(x, ym, ya, kx, vx, w, B, S, Nm, tm):
    T = B * S
    G = T // tm
    row = lambda i: (jnp.minimum(i, G - 1), 0)
    lag = lambda i: (jnp.maximum(i - 1, 0), 0)
    c2 = lambda i: (0, 0)
    per_b = lambda i: ((jnp.minimum(i, G - 1) * tm) // S, 0)

    def const(a):
        return pl.BlockSpec(a.shape, c2, pipeline_mode=pl.Buffered(1))

    return pl.pallas_call(
        _back_kernel,
        grid=(G + 1,),
        in_specs=[pl.BlockSpec((tm, D_MODEL), row),
                  pl.BlockSpec((tm, M_WIDTH), row),
                  pl.BlockSpec((tm, M_WIDTH), row),
                  const(w["wout"]), const(w["gx"]), const(w["wqx"]), const(w["xqg"]),
                  pl.BlockSpec((Nm, X_WIDTH), per_b),
                  pl.BlockSpec((Nm, X_WIDTH), per_b),
                  const(w["wox"]), const(w["gf"]), const(w["w1"]), const(w["w2"])],
        out_specs=pl.BlockSpec((tm, D_MODEL), lag),
        out_shape=jax.ShapeDtypeStruct((T, D_MODEL), f32),
        scratch_shapes=[pltpu.VMEM((tm, D_MODEL), f32)],
        compiler_params=pltpu.CompilerParams(
            dimension_semantics=("arbitrary",), vmem_limit_bytes=VMEM_LIMIT),
        name="back",
    )(x, ym, ya, w["wout"], w["gx"], w["wqx"], w["xqg"], kx, vx, w["wox"], w["gf"], w["w1"], w["w2"])


def _rope_pad(a):
    z = jnp.zeros(a.shape[:-1] + (ROPE_DIM // 2,), a.dtype)
    return jnp.concatenate([a[..., :32], z, a[..., 32:], z], axis=-1)


def _block_diag_heads(w):
    H = w.shape[0]
    wp = jnp.pad(w, ((0, 0), (0, 0), (0, M_DKP - M_DK)))
    eye = jnp.eye(H, dtype=w.dtype)
    return (wp[:, :, None, :] * eye[:, None, :, None]).reshape(H * M_DV, H * M_DKP)


def _layer_weights(l, norm_mix_g, w_in, conv_w, conv_b, wq_m, wk_m, b_igate, b_fgate, m_out_g,
                   cq_norm_g, ckv_norm_g, w_uq, w_ukv, qk_norm_q, qk_norm_k, a_out_g, w_out,
                   norm_x_g, norm_mem_g, wq_x, wkv_x, xq_norm_g, xk_norm_g, wo_x,
                   norm_ffn_g, w_ff1, w_ff2):
    wi = w_in[l].astype(bf16)
    o_v, o_o, o_g = M_WIDTH, 2 * M_WIDTH, 3 * M_WIDTH
    o_f = o_g + M_HEADS
    o_cq = o_f + M_HEADS
    o_ckv = o_cq + Q_LORA
    o_kr = o_ckv + KV_LORA
    wnat = jnp.concatenate([wi[:, :o_v], wi[:, o_o:o_g], wi[:, o_cq:o_kr], _rope_pad(wi[:, o_kr:])], axis=1)
    zg = jnp.zeros((8 - M_HEADS, D_MODEL), wi.dtype)
    wtr = jnp.concatenate([wi[:, o_v:o_o].T, wi[:, o_g:o_f].T, zg, wi[:, o_f:o_cq].T, zg], axis=0)
    z4 = jnp.zeros((8 - M_HEADS,), f32)
    gbias = jnp.concatenate([b_igate[l], z4, b_fgate[l], z4])[:, None]
    wuq = w_uq[l].astype(bf16).reshape(Q_LORA, A_HEADS, A_QK)
    wuq = jnp.concatenate([wuq[..., :NOPE_DIM], _rope_pad(wuq[..., NOPE_DIM:])], axis=-1)
    wukv = w_ukv[l].astype(bf16).reshape(KV_LORA, A_HEADS, NOPE_DIM + A_DV)
    return {
        "gmix": norm_mix_g[l][None, :],
        "wnat": wnat,
        "wtr": wtr,
        "convw": conv_w[l],
        "convb": conv_b[l][None, :],
        "wqt": _block_diag_heads((wq_m[l] * (M_DK ** -0.5)).astype(bf16)).T,
        "wk": _block_diag_heads(wk_m[l].astype(bf16)),
        "gbias": gbias,
        "mog": m_out_g[l].reshape(1, M_WIDTH),
        "cqg": cq_norm_g[l][None, :],
        "ckvg": ckv_norm_g[l][None, :],
        "wuq": wuq.reshape(Q_LORA, A_HEADS * QK_PAD),
        "wukvk": wukv[..., :NOPE_DIM].reshape(KV_LORA, A_HEADS * NOPE_DIM),
        "wukvvt": wukv[..., NOPE_DIM:].reshape(KV_LORA, A_HEADS * A_DV).T,
        "gq": jnp.concatenate([qk_norm_q[l][:NOPE_DIM], _rope_pad(qk_norm_q[l][NOPE_DIM:])])[None, :],
        "gk": jnp.concatenate([qk_norm_k[l][:NOPE_DIM], _rope_pad(qk_norm_k[l][NOPE_DIM:])])[None, :],
        "aog": a_out_g[l].reshape(A_HEADS * A_DV, 1),
        "wout": w_out[l].astype(bf16),
        "gx": norm_x_g[l][None, :],
        "gmem": norm_mem_g[l][None, :],
        "wqx": wq_x[l].astype(bf16),
        "wkvx": wkv_x[l].astype(bf16),
        "xqg": xq_norm_g[l][None, :],
        "xkg": xk_norm_g[l][None, :],
        "wox": wo_x[l].astype(bf16),
        "gf": norm_ffn_g[l][None, :],
        "w1": w_ff1[l].astype(bf16),
        "w2": w_ff2[l].astype(bf16),
    }


def kernel(x, mem, positions, norm_mix_g, w_in, conv_w, conv_b, wq_m, wk_m, b_igate, b_fgate, m_out_g, cq_norm_g, ckv_norm_g, w_uq, w_ukv, qk_norm_q, qk_norm_k, a_out_g, w_out, norm_x_g, norm_mem_g, wq_x, wkv_x, xq_norm_g, xk_norm_g, wo_x, norm_ffn_g, w_ff1, w_ff2):
    B, S, D = x.shape
    Nm = mem.shape[1]
    depth = w_in.shape[0]
    assert D == D_MODEL and S % (2 * SUB) == 0
    tm_back = SUB
    params = (norm_mix_g, w_in, conv_w, conv_b, wq_m, wk_m, b_igate, b_fgate, m_out_g,
              cq_norm_g, ckv_norm_g, w_uq, w_ukv, qk_norm_q, qk_norm_k, a_out_g, w_out,
              norm_x_g, norm_mem_g, wq_x, wkv_x, xq_norm_g, xk_norm_g, wo_x,
              norm_ffn_g, w_ff1, w_ff2)

    cos, sin = _rope_tables(positions)
    xt = x.reshape(B * S, D)
    mem2d = mem.reshape(B * Nm, D)
    for l in range(depth):
        w = _layer_weights(l, *params)
        ym, q, k, vt = _front(xt, cos, sin, w, B, S)
        ya = _attention(q, k, vt, w["aog"], B, S, ATT_T)
        kx, vx = _memkv(mem2d, w["gmem"], w["wkvx"], w["xkg"], B, Nm)
        xt = _back(xt, ym, ya, kx, vx, w, B, S, Nm, tm_back)
    return xt.reshape(B, S, D)
```

```python
import functools

import jax
import jax.numpy as jnp
from jax import lax
from jax.experimental import pallas as pl
from jax.experimental.pallas import tpu as pltpu

f32 = jnp.float32
bf16 = jnp.bfloat16

EPS = 1e-6
D_MODEL = 1024
M_HEADS = 4
M_WIDTH = 512
M_DV = 128
M_DK = 64
CONV_K = 4
A_HEADS = 4
A_DV = 128
Q_LORA = 256
KV_LORA = 128
NOPE_DIM = 128
ROPE_DIM = 64
A_QK = NOPE_DIM + ROPE_DIM
ROPE_THETA = 10000.0
X_HEADS = 4
X_HD = 128
X_WIDTH = 512
D_FF = 4096
FF_CHUNK = 1024

M_CHUNK = 128
M_DKP = 128
SUB = 512
ATT_T = 256
ATT_QT = 4
QK_PAD = 256
ONES_ROWS = 16
N_U, N_O, N_CQ, N_CKV, N_KR, N_END = 0, 512, 1024, 1280, 1408, 1536
T_V, T_GI, T_GF, T_END = 0, 512, 520, 528

VMEM_LIMIT = 56 * 1024 * 1024
LOG2E = 1.4426950408889634


def _dot(a, b):
    return jnp.dot(a, b, preferred_element_type=f32)


def _dot_nt(a, b):
    return lax.dot_general(a, b, (((1,), (1,)), ((), ())), preferred_element_type=f32)


def _rms(x, g, n=None):
    n = x.shape[-1] if n is None else n
    ms = jnp.sum(x * x, axis=-1, keepdims=True) * (1.0 / n)
    return x * lax.rsqrt(ms + EPS) * g


def _sigmoid(x):
    return 1.0 / (1.0 + jnp.exp(-x))


def _log_sigmoid(x):
    return jnp.minimum(x, 0.0) - jnp.log1p(jnp.exp(-jnp.abs(x)))


def _split3(x):
    a = x.astype(bf16)
    r = x - a.astype(f32)
    b = r.astype(bf16)
    c = (r - b.astype(f32)).astype(bf16)
    return a, b, c


def _rope_kernel(pos_ref, inv_ref, sgn_ref, cos_ref, sin_ref):
    ang = pos_ref[...].astype(f32) * inv_ref[...]
    cos_ref[...] = jnp.cos(ang) * jnp.abs(sgn_ref[...])
    sin_ref[...] = jnp.sin(ang) * sgn_ref[...]


def _rope_tables(positions):
    T = positions.size
    tm = 2 * SUB
    assert T % tm == 0
    inv = 1.0 / (ROPE_THETA ** (jnp.arange(0, ROPE_DIM, 2, dtype=f32) / ROPE_DIM))
    z = jnp.zeros((ROPE_DIM // 2,), f32)
    o = jnp.ones((ROPE_DIM // 2,), f32)
    inv_p = jnp.concatenate([inv, z, inv, z])[None, :]
    sgn = jnp.concatenate([-o, z, o, z])[None, :]
    return pl.pallas_call(
        _rope_kernel,
        grid=(T // tm,),
        in_specs=[pl.BlockSpec((tm, 1), lambda i: (i, 0)),
                  pl.BlockSpec((1, 128), lambda i: (0, 0)),
                  pl.BlockSpec((1, 128), lambda i: (0, 0))],
        out_specs=[pl.BlockSpec((tm, 128), lambda i: (i, 0)),
                   pl.BlockSpec((tm, 128), lambda i: (i, 0))],
        out_shape=[jax.ShapeDtypeStruct((T, 128), f32)] * 2,
        name="rope_tables",
    )(positions.reshape(T, 1), inv_p, sgn)


def _front_kernel(x_ref, cos_ref, sin_ref, gmix_ref, wnat_ref, wtr_ref, convw_ref, convb_ref,
                  wqt_ref, wk_ref, gbias_ref, mog_ref, cqg_ref, ckvg_ref,
                  wuq_ref, wukvk_ref, wukvvt_ref, gq_ref, gk_ref,
                  ym_ref, q_ref, k_ref, vt_ref,
                  ubuf_ref, st_ref, mst_ref, lhs_ref, rhs_ref, num_ref, mj_ref, gate_ref,
                  sog_ref, spt_ref, sqt_ref, sk_ref, sqa_ref, skn_ref, skr_ref, *, nt):
    L = M_CHUNK
    NC = SUB // L
    g = pl.program_id(0)

    @pl.when(g == 0)
    def _():
        for ref in (ubuf_ref, st_ref, mst_ref, num_ref, mj_ref, gate_ref,
                    sog_ref, spt_ref, sqt_ref, sk_ref, sqa_ref, skn_ref, skr_ref):
            ref[...] = jnp.zeros(ref.shape, ref.dtype)

    keep_conv = jnp.where(g % nt != 0, 1.0, 0.0).astype(f32)
    keep_state = jnp.where((g + nt - 1) % nt != 0, 1.0, 0.0).astype(f32)

    srow = lax.broadcasted_iota(jnp.int32, (L, L), 0)
    qcol = lax.broadcasted_iota(jnp.int32, (L, L), 1)
    causal_t = srow <= qcol
    triu = jnp.where(causal_t, 1.0, 0.0).astype(bf16)
    ones_f = jnp.ones((ONES_ROWS, L), f32)
    ones_b = jnp.ones((ONES_ROWS, L), bf16)
    zpad = jnp.zeros((L - 8, L), f32)
    items = [(c, hh) for c in range(NC) for hh in range(M_HEADS)]

    h = _rms(x_ref[...], gmix_ref[...]).astype(bf16)
    pn = _dot(h, wnat_ref[...])
    pt = _dot_nt(wtr_ref[...], h)

    def slices(c, hh):
        cols = slice(c * L, (c + 1) * L)
        return (sk_ref[c * L:(c + 1) * L, hh * M_DKP:(hh + 1) * M_DKP],
                sqt_ref[hh * M_DKP:(hh + 1) * M_DKP, cols],
                spt_ref[T_V + hh * M_DV:T_V + (hh + 1) * M_DV, cols])

    scores = [_dot(*slices(c, hh)[0:2]) for c, hh in items]
    gi = spt_ref[T_GI:T_GI + 8, :] + gbias_ref[0:8, :]
    logf = _log_sigmoid(spt_ref[T_GF:T_GF + 8, :] + gbias_ref[8:16, :])
    m_run = mst_ref[:, 0:1] * keep_state
    gts = []
    for c in range(NC):
        cols = slice(c * L, (c + 1) * L)
        bsum = _dot(jnp.concatenate(_split3(logf[:, cols]), axis=0), triu)
        b = bsum[0:8] + bsum[8:16] + bsum[16:24]
        ic = gi[:, cols]
        btot = b[:, L - 1:L]
        a = btot - b + ic
        mloc = jnp.max(a, axis=1, keepdims=True)
        wa = jnp.exp(a - mloc)
        m_prev = m_run
        m_run = jnp.maximum(btot + m_prev, mloc)
        s_old = jnp.exp(btot + m_prev - m_run)
        s_loc = jnp.exp(mloc - m_run)
        r_t = jnp.concatenate([ic - b, zpad], axis=0).T
        gts.append((b, wa, s_old, s_loc, r_t, b + m_prev))
    mst_ref[...] = jnp.broadcast_to(m_run, (8, 128))

    cqn = _rms(pn[:, N_CQ:N_CQ + Q_LORA], cqg_ref[...]).astype(bf16)
    qa = _dot(cqn, wuq_ref[...])
    ckvn = _rms(pn[:, N_CKV:N_CKV + KV_LORA], ckvg_ref[...]).astype(bf16)
    kn = _dot(ckvn, wukvk_ref[...])
    vt = _dot_nt(wukvvt_ref[...], ckvn).astype(bf16)
    for c in range(SUB // ATT_T):
        vt_ref[c * A_HEADS * A_DV:(c + 1) * A_HEADS * A_DV, :] = vt[:, c * ATT_T:(c + 1) * ATT_T]
    u = pn[:, N_U:N_U + M_WIDTH]
    ubuf_ref[0:8, :] = ubuf_ref[0:8, :] * keep_conv
    ubuf_ref[8:8 + SUB, :] = u
    acc = convb_ref[...] + convw_ref[CONV_K - 1:CONV_K, :] * u
    for j in range(CONV_K - 1):
        off = 8 - (CONV_K - 1) + j
        acc = acc + convw_ref[j:j + 1, :] * ubuf_ref[off:off + SUB, :]
    ubuf_ref[0:8, :] = u[SUB - 8:SUB, :]
    ucb = (acc * _sigmoid(acc)).astype(bf16)
    qt_all = _dot_nt(wqt_ref[...], ucb).astype(bf16)
    k_all = _dot(ucb, wk_ref[...]).astype(bf16)

    for i, (c, hh) in enumerate(items):
        hs = slice(hh * M_DV, (hh + 1) * M_DV)
        num = num_ref[i]
        den = num[M_DV:M_DV + 1, :]
        hm = num[0:M_DV, :] / jnp.maximum(jnp.abs(den), jnp.exp(-mj_ref[i][0:1, :]))
        hn = hm * lax.rsqrt(jnp.sum(hm * hm, axis=0, keepdims=True) * (1.0 / M_DV) + EPS)
        ym_ref[c * L:(c + 1) * L, hs] = (hn.T * gate_ref[c * L:(c + 1) * L, hs]).astype(bf16)

    states = [st_ref[hh] * keep_state for hh in range(M_HEADS)]
    for i, (c, hh) in enumerate(items):
        b, wa, s_old, s_loc, r_t, inter_all = gts[c]
        k_c, qt_c, vt_c = slices(c, hh)
        st = states[hh]
        vt_w = jnp.concatenate([vt_c, ones_f], axis=0) * wa[hh:hh + 1, :]
        loc = _dot(vt_w.astype(bf16), k_c)
        states[hh] = s_old[hh:hh + 1, :] * st + s_loc[hh:hh + 1, :] * loc
        inter = inter_all[hh:hh + 1, :]
        dlog = jnp.where(causal_t, r_t[:, hh:hh + 1] + b[hh:hh + 1, :], -jnp.inf)
        mj = jnp.maximum(inter, jnp.max(dlog, axis=0, keepdims=True))
        qk = (scores[i] * jnp.exp(dlog - mj)).astype(bf16)
        s_int = jnp.exp(inter - mj)
        rhs_ref[i] = jnp.concatenate([(qt_c.astype(f32) * s_int).astype(bf16), qk], axis=0)
        lhs_ref[i] = jnp.concatenate(
            [st.astype(bf16), jnp.concatenate([vt_c.astype(bf16), ones_b], axis=0)], axis=1)
        mj_ref[i] = jnp.broadcast_to(mj, (8, L))
    for hh in range(M_HEADS):
        st_ref[hh] = states[hh]
    for i in range(len(items)):
        num_ref[i] = _dot(lhs_ref[i], rhs_ref[i])
    gate_ref[...] = mog_ref[...] * _sigmoid(sog_ref[...])

    cos = cos_ref[...]
    sin = sin_ref[...]

    def rope(t):
        return t * cos + pltpu.roll(t, 64, 1) * sin

    qscale = A_QK ** -0.5 * LOG2E
    gq_n, gq_r = gq_ref[:, 0:128] * qscale, gq_ref[:, 128:256] * qscale
    gk_n, gk_r = gk_ref[:, 0:128], gk_ref[:, 128:256]
    kr = rope(_rms(skr_ref[...], gk_r, ROPE_DIM)).astype(bf16)
    for hh in range(A_HEADS):
        c0 = hh * QK_PAD
        qn = _rms(sqa_ref[:, c0:c0 + 128], gq_n)
        qr = rope(_rms(sqa_ref[:, c0 + 128:c0 + 256], gq_r, ROPE_DIM))
        q_ref[:, c0:c0 + 128] = qn.astype(bf16)
        q_ref[:, c0 + 128:c0 + 256] = qr.astype(bf16)
        k_ref[:, c0:c0 + 128] = _rms(skn_ref[:, hh * 128:(hh + 1) * 128], gk_n).astype(bf16)
        k_ref[:, c0 + 128:c0 + 256] = kr

    sog_ref[...] = pn[:, N_O:N_O + M_WIDTH]
    spt_ref[...] = pt
    sqt_ref[...] = qt_all
    sk_ref[...] = k_all
    sqa_ref[...] = qa
    skn_ref[...] = kn
    skr_ref[...] = pn[:, N_KR:N_KR + 128]


def _front(x, cos, sin, w, B, S):
    tm = SUB
    nt = S // tm
    G = B * nt
    nch = (SUB // M_CHUNK) * M_HEADS
    vt_rows = (tm // ATT_T) * A_HEADS * A_DV

    def lag(d):
        return lambda g: (jnp.clip(g - d, 0, G - 1), 0)

    c2 = lambda g: (0, 0)

    def full(a):
        return pl.BlockSpec(a.shape, c2)

    weights = [w["gmix"], w["wnat"], w["wtr"], w["convw"], w["convb"], w["wqt"], w["wk"], w["gbias"],
               w["mog"], w["cqg"], w["ckvg"], w["wuq"], w["wukvk"], w["wukvvt"], w["gq"], w["gk"]]
    return pl.pallas_call(
        functools.partial(_front_kernel, nt=nt),
        grid=(G + 2,),
        in_specs=[pl.BlockSpec((tm, D_MODEL), lag(0)),
                  pl.BlockSpec((tm, 128), lag(1)),
                  pl.BlockSpec((tm, 128), lag(1))] + [full(a) for a in weights],
        out_specs=[pl.BlockSpec((tm, M_WIDTH), lag(2)),
                   pl.BlockSpec((tm, A_HEADS * QK_PAD), lag(1)),
                   pl.BlockSpec((tm, A_HEADS * QK_PAD), lag(1)),
                   pl.BlockSpec((vt_rows, ATT_T), lag(0))],
        out_shape=[jax.ShapeDtypeStruct((G * tm, M_WIDTH), bf16),
                   jax.ShapeDtypeStruct((G * tm, A_HEADS * QK_PAD), bf16),
                   jax.ShapeDtypeStruct((G * tm, A_HEADS * QK_PAD), bf16),
                   jax.ShapeDtypeStruct((G * vt_rows, ATT_T), bf16)],
        scratch_shapes=[pltpu.VMEM((SUB + 8, M_WIDTH), f32),
                        pltpu.VMEM((M_HEADS, M_DV + ONES_ROWS, M_DKP), f32),
                        pltpu.VMEM((8, 128), f32),
                        pltpu.VMEM((nch, M_DV + ONES_ROWS, M_DKP + M_CHUNK), bf16),
                        pltpu.VMEM((nch, M_DKP + M_CHUNK, M_CHUNK), bf16),
                        pltpu.VMEM((nch, M_DV + ONES_ROWS, M_CHUNK), f32),
                        pltpu.VMEM((nch, 8, M_CHUNK), f32),
                        pltpu.VMEM((SUB, M_WIDTH), f32),
                        pltpu.VMEM((SUB, M_WIDTH), f32),
                        pltpu.VMEM((T_END, SUB), f32),
                        pltpu.VMEM((M_HEADS * M_DKP, SUB), bf16),
                        pltpu.VMEM((SUB, M_HEADS * M_DKP), bf16),
                        pltpu.VMEM((SUB, A_HEADS * QK_PAD), f32),
                        pltpu.VMEM((SUB, A_HEADS * NOPE_DIM), f32),
                        pltpu.VMEM((SUB, 128), f32)],
        compiler_params=pltpu.CompilerParams(
            dimension_semantics=("arbitrary",), vmem_limit_bytes=VMEM_LIMIT),
        name="front",
    )(x, cos, sin, *weights)


def _attn_kernel(q_ref, k_ref, vt_ref, g_ref, o_ref, acc_ref, *, tq):
    ones = jnp.ones((ONES_ROWS, tq), bf16)
    for t in range(ATT_QT):
        qi = pl.program_id(1) * ATT_QT + t
        rows = slice(t * tq, (t + 1) * tq)
        acc_ref[...] = jnp.zeros(acc_ref.shape, f32)

        def block(j, ms, masked, rows=rows):
            r0 = pl.multiple_of(j * tq, tq)
            v0 = j * (A_HEADS * A_DV)
            ss = [_dot_nt(k_ref[pl.ds(r0, tq), hh * QK_PAD:(hh + 1) * QK_PAD],
                          q_ref[rows, hh * QK_PAD:(hh + 1) * QK_PAD]) for hh in range(A_HEADS)]
            out = []
            for hh in range(A_HEADS):
                s = ss[hh]
                if masked:
                    row = lax.broadcasted_iota(jnp.int32, (tq, tq), 0)
                    col = lax.broadcasted_iota(jnp.int32, (tq, tq), 1)
                    s = jnp.where(row <= col, s, -jnp.inf)
                m_new = jnp.maximum(ms[hh], jnp.max(s, axis=0, keepdims=True))
                alpha = jnp.exp2(ms[hh] - m_new)
                p = jnp.exp2(s - m_new).astype(bf16)
                vte = jnp.concatenate(
                    [vt_ref[pl.ds(pl.multiple_of(v0 + hh * A_DV, A_DV), A_DV), :], ones], axis=0)
                acc_ref[hh] = alpha * acc_ref[hh] + _dot(vte, p)
                out.append(m_new)
            return tuple(out)

        ms = tuple(jnp.full((1, tq), -jnp.inf, f32) for _ in range(A_HEADS))
        ms = lax.fori_loop(0, qi, lambda j, c: block(j, c, False), ms)
        block(qi, ms, True)
        for hh in range(A_HEADS):
            a = acc_ref[hh]
            o = a[0:A_DV, :] / a[A_DV:A_DV + 1, :]
            ms2 = jnp.sum(o * o, axis=0, keepdims=True) * (1.0 / A_DV)
            y = o * lax.rsqrt(ms2 + EPS) * g_ref[hh * A_DV:(hh + 1) * A_DV, :]
            o_ref[rows, hh * A_DV:(hh + 1) * A_DV] = y.T.astype(bf16)


def _attention(q, k, vt, g, B, S, tq):
    T = B * S
    nq = S // tq
    ng = nq // ATT_QT
    return pl.pallas_call(
        functools.partial(_attn_kernel, tq=tq),
        grid=(B, ng),
        in_specs=[pl.BlockSpec((ATT_QT * tq, A_HEADS * QK_PAD), lambda b, i: (b * ng + i, 0)),
                  pl.BlockSpec((S, A_HEADS * QK_PAD), lambda b, i: (b, 0)),
                  pl.BlockSpec((nq * A_HEADS * A_DV, tq), lambda b, i: (b, 0)),
                  pl.BlockSpec((A_HEADS * A_DV, 1), lambda b, i: (0, 0))],
        out_specs=pl.BlockSpec((ATT_QT * tq, A_HEADS * A_DV), lambda b, i: (b * ng + i, 0)),
        out_shape=jax.ShapeDtypeStruct((T, A_HEADS * A_DV), bf16),
        scratch_shapes=[pltpu.VMEM((A_HEADS, A_DV + ONES_ROWS, tq), f32)],
        compiler_params=pltpu.CompilerParams(
            dimension_semantics=("arbitrary", "arbitrary"), vmem_limit_bytes=VMEM_LIMIT),
        name="mla_attention",
    )(q, k, vt, g)


def _memkv_kernel(mem_ref, g_ref, w_ref, kg_ref, k_ref, v_ref):
    mn = _rms(mem_ref[...], g_ref[...]).astype(bf16)
    kv = _dot(mn, w_ref[...])
    for hh in range(X_HEADS):
        sl = slice(hh * X_HD, (hh + 1) * X_HD)
        k_ref[:, sl] = _rms(kv[:, sl], kg_ref[...]).astype(bf16)
    v_ref[...] = kv[:, X_WIDTH:].astype(bf16)


def _memkv(mem2d, g, wkv, kg, B, Nm):
    return pl.pallas_call(
        _memkv_kernel,
        grid=(B,),
        in_specs=[pl.BlockSpec((Nm, D_MODEL), lambda b: (b, 0)),
                  pl.BlockSpec((1, D_MODEL), lambda b: (0, 0)),
                  pl.BlockSpec((D_MODEL, 2 * X_WIDTH), lambda b: (0, 0)),
                  pl.BlockSpec((1, X_HD), lambda b: (0, 0))],
        out_specs=[pl.BlockSpec((Nm, X_WIDTH), lambda b: (b, 0)),
                   pl.BlockSpec((Nm, X_WIDTH), lambda b: (b, 0))],
        out_shape=[jax.ShapeDtypeStruct((B * Nm, X_WIDTH), bf16)] * 2,
        compiler_params=pltpu.CompilerParams(
            dimension_semantics=("arbitrary",), vmem_limit_bytes=VMEM_LIMIT),
        name="mem_kv",
    )(mem2d, g, wkv, kg)


def _back_kernel(x_ref, ym_ref, ya_ref, wout_ref, gx_ref, wqx_ref, xqg_ref, kx_ref, vx_ref,
                 wox_ref, gf_ref, w1_ref, w2_ref, out_ref, x2_ref):
    @pl.when(pl.program_id(0) == 0)
    def _():
        x2_ref[...] = jnp.zeros(x2_ref.shape, f32)

    x2p = x2_ref[...]
    hf = _rms(x2p, gf_ref[...]).astype(bf16)
    mlp = {"acc": x2p, "a": None}

    def mlp_step(k):
        c, second = divmod(k, 2)
        sl = slice(c * FF_CHUNK, (c + 1) * FF_CHUNK)
        if not second:
            a = jnp.maximum(_dot(hf, w1_ref[:, sl]), 0.0)
            mlp["a"] = (a * a).astype(bf16)
        else:
            mlp["acc"] = mlp["acc"] + _dot(mlp["a"], w2_ref[sl, :])

    nsteps = 2 * (D_FF // FF_CHUNK)
    x1 = (x_ref[...] + _dot(ym_ref[...], wout_ref[0:M_WIDTH, :])
          + _dot(ya_ref[...], wout_ref[M_WIDTH:2 * M_WIDTH, :]))
    mlp_step(0)
    hn = _rms(x1, gx_ref[...]).astype(bf16)
    qx = _dot(hn, wqx_ref[...])
    mlp_step(1)
    ps = []
    for hh in range(X_HEADS):
        sl = slice(hh * X_HD, (hh + 1) * X_HD)
        qh = (_rms(qx[:, sl], xqg_ref[...]) * (X_HD ** -0.5)).astype(bf16)
        s = _dot_nt(qh, kx_ref[:, sl])
        ps.append(jnp.exp(s - jnp.max(s, axis=-1, keepdims=True)))
    mlp_step(2)
    heads = []
    for hh in range(X_HEADS):
        sl = slice(hh * X_HD, (hh + 1) * X_HD)
        o = _dot(ps[hh].astype(bf16), vx_ref[:, sl]) / jnp.sum(ps[hh], axis=-1, keepdims=True)
        heads.append(o.astype(bf16))
    mlp_step(3)
    x2 = x1 + _dot(jnp.concatenate(heads, axis=-1), wox_ref[...])
    for k in range(4, nsteps):
        mlp_step(k)
    out_ref[...] = mlp["acc"]
    x2_ref[...] = x2


def _back(x, ym, ya, kx, vx, w, B, S, Nm, tm):
    T = B * S
    G = T // tm
    row = lambda i: (jnp.minimum(i, G - 1), 0)
    lag = lambda i: (jnp.maximum(i - 1, 0), 0)
    c2 = lambda i: (0, 0)
    per_b = lambda i: ((jnp.minimum(i, G - 1) * tm) // S, 0)

    def const(a):
        return pl.BlockSpec(a.shape, c2, pipeline_mode=pl.Buffered(1))

    return pl.pallas_call(
        _back_kernel,
        grid=(G + 1,),
        in_specs=[pl.BlockSpec((tm, D_MODEL), row),
                  pl.BlockSpec((tm, M_WIDTH), row),
                  pl.BlockSpec((tm, M_WIDTH), row),
                  const(w["wout"]), const(w["gx"]), const(w["wqx"]), const(w["xqg"]),
                  pl.BlockSpec((Nm, X_WIDTH), per_b),
                  pl.BlockSpec((Nm, X_WIDTH), per_b),
                  const(w["wox"]), const(w["gf"]), const(w["w1"]), const(w["w2"])],
        out_specs=pl.BlockSpec((tm, D_MODEL), lag),
        out_shape=jax.ShapeDtypeStruct((T, D_MODEL), f32),
        scratch_shapes=[pltpu.VMEM((tm, D_MODEL), f32)],
        compiler_params=pltpu.CompilerParams(
            dimension_semantics=("arbitrary",), vmem_limit_bytes=VMEM_LIMIT),
        name="back",
    )(x, ym, ya, w["wout"], w["gx"], w["wqx"], w["xqg"], kx, vx, w["wox"], w["gf"], w["w1"], w["w2"])


def _rope_pad(a):
    z = jnp.zeros(a.shape[:-1] + (ROPE_DIM // 2,), a.dtype)
    return jnp.concatenate([a[..., :32], z, a[..., 32:], z], axis=-1)


def _block_diag_heads(w):
    H = w.shape[0]
    wp = jnp.pad(w, ((0, 0), (0, 0), (0, M_DKP - M_DK)))
    eye = jnp.eye(H, dtype=w.dtype)
    return (wp[:, :, None, :] * eye[:, None, :, None]).reshape(H * M_DV, H * M_DKP)


def _layer_weights(l, norm_mix_g, w_in, conv_w, conv_b, wq_m, wk_m, b_igate, b_fgate, m_out_g,
                   cq_norm_g, ckv_norm_g, w_uq, w_ukv, qk_norm_q, qk_norm_k, a_out_g, w_out,
                   norm_x_g, norm_mem_g, wq_x, wkv_x, xq_norm_g, xk_norm_g, wo_x,
                   norm_ffn_g, w_ff1, w_ff2):
    wi = w_in[l].astype(bf16)
    o_v, o_o, o_g = M_WIDTH, 2 * M_WIDTH, 3 * M_WIDTH
    o_f = o_g + M_HEADS
    o_cq = o_f + M_HEADS
    o_ckv = o_cq + Q_LORA
    o_kr = o_ckv + KV_LORA
    wnat = jnp.concatenate([wi[:, :o_v], wi[:, o_o:o_g], wi[:, o_cq:o_kr], _rope_pad(wi[:, o_kr:])], axis=1)
    zg = jnp.zeros((8 - M_HEADS, D_MODEL), wi.dtype)
    wtr = jnp.concatenate([wi[:, o_v:o_o].T, wi[:, o_g:o_f].T, zg, wi[:, o_f:o_cq].T, zg], axis=0)
    z4 = jnp.zeros((8 - M_HEADS,), f32)
    gbias = jnp.concatenate([b_igate[l], z4, b_fgate[l], z4])[:, None]
    wuq = w_uq[l].astype(bf16).reshape(Q_LORA, A_HEADS, A_QK)
    wuq = jnp.concatenate([wuq[..., :NOPE_DIM], _rope_pad(wuq[..., NOPE_DIM:])], axis=-1)
    wukv = w_ukv[l].astype(bf16).reshape(KV_LORA, A_HEADS, NOPE_DIM + A_DV)
    return {
        "gmix": norm_mix_g[l][None, :],
        "wnat": wnat,
        "wtr": wtr,
        "convw": conv_w[l],
        "convb": conv_b[l][None, :],
        "wqt": _block_diag_heads((wq_m[l] * (M_DK ** -0.5)).astype(bf16)).T,
        "wk": _block_diag_heads(wk_m[l].astype(bf16)),
        "gbias": gbias,
        "mog": m_out_g[l].reshape(1, M_WIDTH),
        "cqg": cq_norm_g[l][None, :],
        "ckvg": ckv_norm_g[l][None, :],
        "wuq": wuq.reshape(Q_LORA, A_HEADS * QK_PAD),
        "wukvk": wukv[..., :NOPE_DIM].reshape(KV_LORA, A_HEADS * NOPE_DIM),
        "wukvvt": wukv[..., NOPE_DIM:].reshape(KV_LORA, A_HEADS * A_DV).T,
        "gq": jnp.concatenate([qk_norm_q[l][:NOPE_DIM], _rope_pad(qk_norm_q[l][NOPE_DIM:])])[None, :],
        "gk": jnp.concatenate([qk_norm_k[l][:NOPE_DIM], _rope_pad(qk_norm_k[l][NOPE_DIM:])])[None, :],
        "aog": a_out_g[l].reshape(A_HEADS * A_DV, 1),
        "wout": w_out[l].astype(bf16),
        "gx": norm_x_g[l][None, :],
        "gmem": norm_mem_g[l][None, :],
        "wqx": wq_x[l].astype(bf16),
        "wkvx": wkv_x[l].astype(bf16),
        "xqg": xq_norm_g[l][None, :],
        "xkg": xk_norm_g[l][None, :],
        "wox": wo_x[l].astype(bf16),
        "gf": norm_ffn_g[l][None, :],
        "w1": w_ff1[l].astype(bf16),
        "w2": w_ff2[l].astype(bf16),
    }


def kernel(x, mem, positions, norm_mix_g, w_in, conv_w, conv_b, wq_m, wk_m, b_igate, b_fgate, m_out_g, cq_norm_g, ckv_norm_g, w_uq, w_ukv, qk_norm_q, qk_norm_k, a_out_g, w_out, norm_x_g, norm_mem_g, wq_x, wkv_x, xq_norm_g, xk_norm_g, wo_x, norm_ffn_g, w_ff1, w_ff2):
    B, S, D = x.shape
    Nm = mem.shape[1]
    depth = w_in.shape[0]
    assert D == D_MODEL and S % (2 * SUB) == 0
    tm_back = SUB
    params = (norm_mix_g, w_in, conv_w, conv_b, wq_m, wk_m, b_igate, b_fgate, m_out_g,
              cq_norm_g, ckv_norm_g, w_uq, w_ukv, qk_norm_q, qk_norm_k, a_out_g, w_out,
              norm_x_g, norm_mem_g, wq_x, wkv_x, xq_norm_g, xk_norm_g, wo_x,
              norm_ffn_g, w_ff1, w_ff2)

    cos, sin = _rope_tables(positions)
    xt = x.reshape(B * S, D)
    mem2d = mem.reshape(B * Nm, D)
    for l in range(depth):
        w = _layer_weights(l, *params)
        ym, q, k, vt = _front(xt, cos, sin, w, B, S)
        ya = _attention(q, k, vt, w["aog"], B, S, ATT_T)
        kx, vx = _memkv(mem2d, w["gmem"], w["wkvx"], w["xkg"], B, Nm)
        xt = _back(xt, ym, ya, kx, vx, w, B, S, Nm, tm_back)
    return xt.reshape(B, S, D)
```

```python
import functools
import math

import jax
import jax.numpy as jnp
from jax import lax
from jax.experimental import pallas as pl
from jax.experimental.pallas import tpu as pltpu

f32 = jnp.float32
bf16 = jnp.bfloat16

EPS = 1e-6
D_MODEL = 1024
M_HEADS = 4
M_WIDTH = 512
M_DV = 128
M_DK = 64
CONV_K = 4
A_HEADS = 4
A_DV = 128
Q_LORA = 256
KV_LORA = 128
NOPE_DIM = 128
ROPE_DIM = 64
A_QK = NOPE_DIM + ROPE_DIM
ROPE_THETA = 10000.0
X_HEADS = 4
X_HD = 128
X_WIDTH = 512
D_FF = 4096
FF_CHUNK = 1024

M_CHUNK = 128
M_DKP = 128
SUB = 512
ATT_T = 256
ATT_QT = 4
QK_PAD = 256
ONES_ROWS = 16
N_U, N_O, N_CQ, N_CKV, N_KR, N_END = 0, 512, 1024, 1280, 1408, 1536
T_V, T_GI, T_GF, T_END = 0, 512, 520, 528

ROPE_PACK = 4

VMEM_LIMIT = 56 * 1024 * 1024
LOG2E = 1.4426950408889634


def _dot(a, b):
    return jnp.dot(a, b, preferred_element_type=f32)


def _dot_nt(a, b):
    return lax.dot_general(a, b, (((1,), (1,)), ((), ())), preferred_element_type=f32)


def _rms(x, g, n=None):
    n = x.shape[-1] if n is None else n
    ms = jnp.sum(x * x, axis=-1, keepdims=True) * (1.0 / n)
    return x * lax.rsqrt(ms + EPS) * g


def _sigmoid(x):
    return 1.0 / (1.0 + jnp.exp(-x))


def _log_sigmoid(x):
    return jnp.minimum(x, 0.0) - jnp.log1p(jnp.exp(-jnp.abs(x)))


def _split3(x):
    a = x.astype(bf16)
    r = x - a.astype(f32)
    b = r.astype(bf16)
    c = (r - b.astype(f32)).astype(bf16)
    return a, b, c


def _rope_kernel(pos_ref, inv_ref, cos_ref, sin_ref):
    tm = pos_ref.shape[0]
    half = ROPE_DIM // 2
    lane = lax.broadcasted_iota(jnp.int32, (tm, 128), 1)
    pos = jnp.zeros((tm, 128), f32)
    for p in range(ROPE_PACK):
        pos = jnp.where(lane // half == p, pos_ref[:, p:p + 1].astype(f32), pos)
    ang = pos * inv_ref[...]
    cosv = jnp.cos(ang)
    sinv = jnp.sin(ang)
    first = lane < half
    for p in range(ROPE_PACK):
        c = jnp.where(first, pltpu.roll(cosv, (128 - p * half) % 128, 1), 0.0)
        sn = jnp.where(first, pltpu.roll(sinv, (128 - p * half) % 128, 1), 0.0)
        cos_ref[p] = c + pltpu.roll(c, 64, 1)
        sin_ref[p] = pltpu.roll(sn, 64, 1) - sn


def _rope_tables(positions):
    T = positions.size
    rows = T // ROPE_PACK
    tm = math.gcd(rows, 2 * SUB)
    assert T % ROPE_PACK == 0 and tm % 8 == 0
    inv = 1.0 / (ROPE_THETA ** (jnp.arange(0, ROPE_DIM, 2, dtype=f32) / ROPE_DIM))
    inv_p = jnp.tile(inv, ROPE_PACK)[None, :]
    pos_p = positions.reshape(ROPE_PACK, rows).T
    cos, sin = pl.pallas_call(
        _rope_kernel,
        grid=(rows // tm,),
        in_specs=[pl.BlockSpec((tm, ROPE_PACK), lambda i: (i, 0)),
                  pl.BlockSpec((1, 128), lambda i: (0, 0))],
        out_specs=[pl.BlockSpec((ROPE_PACK, tm, 128), lambda i: (0, i, 0)),
                   pl.BlockSpec((ROPE_PACK, tm, 128), lambda i: (0, i, 0))],
        out_shape=[jax.ShapeDtypeStruct((ROPE_PACK, rows, 128), f32)] * 2,
        name="rope_tables",
    )(pos_p, inv_p)
    return cos.reshape(T, 128), sin.reshape(T, 128)


def _front_kernel(x_ref, cos_ref, sin_ref, gmix_ref, wnat_ref, wtr_ref, convw_ref, convb_ref,
                  wqt_ref, wk_ref, gbias_ref, mog_ref, cqg_ref, ckvg_ref,
                  wuq_ref, wukvk_ref, wukvvt_ref, gq_ref, gk_ref,
                  ym_ref, q_ref, k_ref, vt_ref,
                  ubuf_ref, st_ref, mst_ref, lhs_ref, rhs_ref, num_ref, mj_ref, gate_ref,
                  sog_ref, spt_ref, sqt_ref, sk_ref, sqa_ref, skn_ref, skr_ref, *, nt):
    L = M_CHUNK
    NC = SUB // L
    g = pl.program_id(0)

    @pl.when(g == 0)
    def _():
        for ref in (ubuf_ref, st_ref, mst_ref, num_ref, mj_ref, gate_ref,
                    sog_ref, spt_ref, sqt_ref, sk_ref, sqa_ref, skn_ref, skr_ref):
            ref[...] = jnp.zeros(ref.shape, ref.dtype)

    keep_conv = jnp.where(g % nt != 0, 1.0, 0.0).astype(f32)
    keep_state = jnp.where((g + nt - 1) % nt != 0, 1.0, 0.0).astype(f32)

    srow = lax.broadcasted_iota(jnp.int32, (L, L), 0)
    qcol = lax.broadcasted_iota(jnp.int32, (L, L), 1)
    causal_t = srow <= qcol
    triu = jnp.where(causal_t, 1.0, 0.0).astype(bf16)
    ones_f = jnp.ones((ONES_ROWS, L), f32)
    ones_b = jnp.ones((ONES_ROWS, L), bf16)
    zpad = jnp.zeros((L - 8, L), f32)
    items = [(c, hh) for c in range(NC) for hh in range(M_HEADS)]

    h = _rms(x_ref[...], gmix_ref[...]).astype(bf16)
    pn = _dot(h, wnat_ref[...])
    pt = _dot_nt(wtr_ref[...], h)

    def slices(c, hh):
        cols = slice(c * L, (c + 1) * L)
        return (sk_ref[c * L:(c + 1) * L, hh * M_DKP:(hh + 1) * M_DKP],
                sqt_ref[hh * M_DKP:(hh + 1) * M_DKP, cols],
                spt_ref[T_V + hh * M_DV:T_V + (hh + 1) * M_DV, cols])

    scores = [_dot(*slices(c, hh)[0:2]) for c, hh in items]
    gi = spt_ref[T_GI:T_GI + 8, :] + gbias_ref[0:8, :]
    logf = _log_sigmoid(spt_ref[T_GF:T_GF + 8, :] + gbias_ref[8:16, :])
    m_run = mst_ref[:, 0:1] * keep_state
    gts = []
    for c in range(NC):
        cols = slice(c * L, (c + 1) * L)
        bsum = _dot(jnp.concatenate(_split3(logf[:, cols]), axis=0), triu)
        b = bsum[0:8] + bsum[8:16] + bsum[16:24]
        ic = gi[:, cols]
        btot = b[:, L - 1:L]
        a = btot - b + ic
        mloc = jnp.max(a, axis=1, keepdims=True)
        wa = jnp.exp(a - mloc)
        m_prev = m_run
        m_run = jnp.maximum(btot + m_prev, mloc)
        s_old = jnp.exp(btot + m_prev - m_run)
        s_loc = jnp.exp(mloc - m_run)
        r_t = jnp.concatenate([ic - b, zpad], axis=0).T
        gts.append((b, wa, s_old, s_loc, r_t, b + m_prev))
    mst_ref[...] = jnp.broadcast_to(m_run, (8, 128))

    cqn = _rms(pn[:, N_CQ:N_CQ + Q_LORA], cqg_ref[...]).astype(bf16)
    qa = _dot(cqn, wuq_ref[...])
    ckvn = _rms(pn[:, N_CKV:N_CKV + KV_LORA], ckvg_ref[...]).astype(bf16)
    kn = _dot(ckvn, wukvk_ref[...])
    vt = _dot_nt(wukvvt_ref[...], ckvn).astype(bf16)
    for c in range(SUB // ATT_T):
        vt_ref[c * A_HEADS * A_DV:(c + 1) * A_HEADS * A_DV, :] = vt[:, c * ATT_T:(c + 1) * ATT_T]
    u = pn[:, N_U:N_U + M_WIDTH]
    ubuf_ref[0:8, :] = ubuf_ref[0:8, :] * keep_conv
    ubuf_ref[8:8 + SUB, :] = u
    acc = convb_ref[...] + convw_ref[CONV_K - 1:CONV_K, :] * u
    for j in range(CONV_K - 1):
        off = 8 - (CONV_K - 1) + j
        acc = acc + convw_ref[j:j + 1, :] * ubuf_ref[off:off + SUB, :]
    ubuf_ref[0:8, :] = u[SUB - 8:SUB, :]
    ucb = (acc * _sigmoid(acc)).astype(bf16)
    qt_all = _dot_nt(wqt_ref[...], ucb).astype(bf16)
    k_all = _dot(ucb, wk_ref[...]).astype(bf16)

    for i, (c, hh) in enumerate(items):
        hs = slice(hh * M_DV, (hh + 1) * M_DV)
        num = num_ref[i]
        den = num[M_DV:M_DV + 1, :]
        hm = num[0:M_DV, :] / jnp.maximum(jnp.abs(den), jnp.exp(-mj_ref[i][0:1, :]))
        hn = hm * lax.rsqrt(jnp.sum(hm * hm, axis=0, keepdims=True) * (1.0 / M_DV) + EPS)
        ym_ref[c * L:(c + 1) * L, hs] = (hn.T * gate_ref[c * L:(c + 1) * L, hs]).astype(bf16)

    states = [st_ref[hh] * keep_state for hh in range(M_HEADS)]
    for i, (c, hh) in enumerate(items):
        b, wa, s_old, s_loc, r_t, inter_all = gts[c]
        k_c, qt_c, vt_c = slices(c, hh)
        st = states[hh]
        vt_w = jnp.concatenate([vt_c, ones_f], axis=0) * wa[hh:hh + 1, :]
        loc = _dot(vt_w.astype(bf16), k_c)
        states[hh] = s_old[hh:hh + 1, :] * st + s_loc[hh:hh + 1, :] * loc
        inter = inter_all[hh:hh + 1, :]
        dlog = jnp.where(causal_t, r_t[:, hh:hh + 1] + b[hh:hh + 1, :], -jnp.inf)
        mj = jnp.maximum(inter, jnp.max(dlog, axis=0, keepdims=True))
        qk = (scores[i] * jnp.exp(dlog - mj)).astype(bf16)
        s_int = jnp.exp(inter - mj)
        rhs_ref[i] = jnp.concatenate([(qt_c.astype(f32) * s_int).astype(bf16), qk], axis=0)
        lhs_ref[i] = jnp.concatenate(
            [st.astype(bf16), jnp.concatenate([vt_c.astype(bf16), ones_b], axis=0)], axis=1)
        mj_ref[i] = jnp.broadcast_to(mj, (8, L))
    for hh in range(M_HEADS):
        st_ref[hh] = states[hh]
    for i in range(len(items)):
        num_ref[i] = _dot(lhs_ref[i], rhs_ref[i])
    gate_ref[...] = mog_ref[...] * _sigmoid(sog_ref[...])

    cos = cos_ref[...]
    sin = sin_ref[...]

    def rope(t):
        return t * cos + pltpu.roll(t, 64, 1) * sin

    qscale = A_QK ** -0.5 * LOG2E
    gq_n, gq_r = gq_ref[:, 0:128] * qscale, gq_ref[:, 128:256] * qscale
    gk_n, gk_r = gk_ref[:, 0:128], gk_ref[:, 128:256]
    kr = rope(_rms(skr_ref[...], gk_r, ROPE_DIM)).astype(bf16)
    for hh in range(A_HEADS):
        c0 = hh * QK_PAD
        qn = _rms(sqa_ref[:, c0:c0 + 128], gq_n)
        qr = rope(_rms(sqa_ref[:, c0 + 128:c0 + 256], gq_r, ROPE_DIM))
        q_ref[:, c0:c0 + 128] = qn.astype(bf16)
        q_ref[:, c0 + 128:c0 + 256] = qr.astype(bf16)
        k_ref[:, c0:c0 + 128] = _rms(skn_ref[:, hh * 128:(hh + 1) * 128], gk_n).astype(bf16)
        k_ref[:, c0 + 128:c0 + 256] = kr

    sog_ref[...] = pn[:, N_O:N_O + M_WIDTH]
    spt_ref[...] = pt
    sqt_ref[...] = qt_all
    sk_ref[...] = k_all
    sqa_ref[...] = qa
    skn_ref[...] = kn
    skr_ref[...] = pn[:, N_KR:N_KR + 128]


def _front(x, cos, sin, w, B, S):
    tm = SUB
    nt = S // tm
    G = B * nt
    nch = (SUB // M_CHUNK) * M_HEADS
    vt_rows = (tm // ATT_T) * A_HEADS * A_DV

    def lag(d):
        return lambda g: (jnp.clip(g - d, 0, G - 1), 0)

    c2 = lambda g: (0, 0)

    def full(a):
        return pl.BlockSpec(a.shape, c2)

    weights = [w["gmix"], w["wnat"], w["wtr"], w["convw"], w["convb"], w["wqt"], w["wk"], w["gbias"],
               w["mog"], w["cqg"], w["ckvg"], w["wuq"], w["wukvk"], w["wukvvt"], w["gq"], w["gk"]]
    return pl.pallas_call(
        functools.partial(_front_kernel, nt=nt),
        grid=(G + 2,),
        in_specs=[pl.BlockSpec((tm, D_MODEL), lag(0)),
                  pl.BlockSpec((tm, 128), lag(1)),
                  pl.BlockSpec((tm, 128), lag(1))] + [full(a) for a in weights],
        out_specs=[pl.BlockSpec((tm, M_WIDTH), lag(2)),
                   pl.BlockSpec((tm, A_HEADS * QK_PAD), lag(1)),
                   pl.BlockSpec((tm, A_HEADS * QK_PAD), lag(1)),
                   pl.BlockSpec((vt_rows, ATT_T), lag(0))],
        out_shape=[jax.ShapeDtypeStruct((G * tm, M_WIDTH), bf16),
                   jax.ShapeDtypeStruct((G * tm, A_HEADS * QK_PAD), bf16),
                   jax.ShapeDtypeStruct((G * tm, A_HEADS * QK_PAD), bf16),
                   jax.ShapeDtypeStruct((G * vt_rows, ATT_T), bf16)],
        scratch_shapes=[pltpu.VMEM((SUB + 8, M_WIDTH), f32),
                        pltpu.VMEM((M_HEADS, M_DV + ONES_ROWS, M_DKP), f32),
                        pltpu.VMEM((8, 128), f32),
                        pltpu.VMEM((nch, M_DV + ONES_ROWS, M_DKP + M_CHUNK), bf16),
                        pltpu.VMEM((nch, M_DKP + M_CHUNK, M_CHUNK), bf16),
                        pltpu.VMEM((nch, M_DV + ONES_ROWS, M_CHUNK), f32),
                        pltpu.VMEM((nch, 8, M_CHUNK), f32),
                        pltpu.VMEM((SUB, M_WIDTH), f32),
                        pltpu.VMEM((SUB, M_WIDTH), f32),
                        pltpu.VMEM((T_END, SUB), f32),
                        pltpu.VMEM((M_HEADS * M_DKP, SUB), bf16),
                        pltpu.VMEM((SUB, M_HEADS * M_DKP), bf16),
                        pltpu.VMEM((SUB, A_HEADS * QK_PAD), f32),
                        pltpu.VMEM((SUB, A_HEADS * NOPE_DIM), f32),
                        pltpu.VMEM((SUB, 128), f32)],
        compiler_params=pltpu.CompilerParams(
            dimension_semantics=("arbitrary",), vmem_limit_bytes=VMEM_LIMIT),
        name="front",
    )(x, cos, sin, *weights)


def _attn_kernel(q_ref, k_ref, vt_ref, g_ref, o_ref, acc_ref, *, tq):
    ones = jnp.ones((ONES_ROWS, tq), bf16)
    for t in range(ATT_QT):
        qi = pl.program_id(1) * ATT_QT + t
        rows = slice(t * tq, (t + 1) * tq)
        acc_ref[...] = jnp.zeros(acc_ref.shape, f32)

        def block(j, ms, masked, rows=rows):
            r0 = pl.multiple_of(j * tq, tq)
            v0 = j * (A_HEADS * A_DV)
            ss = [_dot_nt(k_ref[pl.ds(r0, tq), hh * QK_PAD:(hh + 1) * QK_PAD],
                          q_ref[rows, hh * QK_PAD:(hh + 1) * QK_PAD]) for hh in range(A_HEADS)]
            out = []
            for hh in range(A_HEADS):
                s = ss[hh]
                if masked:
                    row = lax.broadcasted_iota(jnp.int32, (tq, tq), 0)
                    col = lax.broadcasted_iota(jnp.int32, (tq, tq), 1)
                    s = jnp.where(row <= col, s, -jnp.inf)
                m_new = jnp.maximum(ms[hh], jnp.max(s, axis=0, keepdims=True))
                alpha = jnp.exp2(ms[hh] - m_new)
                p = jnp.exp2(s - m_new).astype(bf16)
                vte = jnp.concatenate(
                    [vt_ref[pl.ds(pl.multiple_of(v0 + hh * A_DV, A_DV), A_DV), :], ones], axis=0)
                acc_ref[hh] = alpha * acc_ref[hh] + _dot(vte, p)
                out.append(m_new)
            return tuple(out)

        ms = tuple(jnp.full((1, tq), -jnp.inf, f32) for _ in range(A_HEADS))
        ms = lax.fori_loop(0, qi, lambda j, c: block(j, c, False), ms)
        block(qi, ms, True)
        for hh in range(A_HEADS):
            a = acc_ref[hh]
            o = a[0:A_DV, :] / a[A_DV:A_DV + 1, :]
            ms2 = jnp.sum(o * o, axis=0, keepdims=True) * (1.0 / A_DV)
            y = o * lax.rsqrt(ms2 + EPS) * g_ref[hh * A_DV:(hh + 1) * A_DV, :]
            o_ref[rows, hh * A_DV:(hh + 1) * A_DV] = y.T.astype(bf16)


def _attention(q, k, vt, g, B, S, tq):
    T = B * S
    nq = S // tq
    ng = nq // ATT_QT
    return pl.pallas_call(
        functools.partial(_attn_kernel, tq=tq),
        grid=(B, ng),
        in_specs=[pl.BlockSpec((ATT_QT * tq, A_HEADS * QK_PAD), lambda b, i: (b * ng + i, 0)),
                  pl.BlockSpec((S, A_HEADS * QK_PAD), lambda b, i: (b, 0)),
                  pl.BlockSpec((nq * A_HEADS * A_DV, tq), lambda b, i: (b, 0)),
                  pl.BlockSpec((A_HEADS * A_DV, 1), lambda b, i: (0, 0))],
        out_specs=pl.BlockSpec((ATT_QT * tq, A_HEADS * A_DV), lambda b, i: (b * ng + i, 0)),
        out_shape=jax.ShapeDtypeStruct((T, A_HEADS * A_DV), bf16),
        scratch_shapes=[pltpu.VMEM((A_HEADS, A_DV + ONES_ROWS, tq), f32)],
        compiler_params=pltpu.CompilerParams(
            dimension_semantics=("arbitrary", "arbitrary"), vmem_limit_bytes=VMEM_LIMIT),
        name="mla_attention",
    )(q, k, vt, g)


def _memkv_kernel(mem_ref, g_ref, w_ref, kg_ref, k_ref, v_ref):
    mn = _rms(mem_ref[...], g_ref[...]).astype(bf16)
    kv = _dot(mn, w_ref[...])
    for hh in range(X_HEADS):
        sl = slice(hh * X_HD, (hh + 1) * X_HD)
        k_ref[:, sl] = _rms(kv[:, sl], kg_ref[...]).astype(bf16)
    v_ref[...] = kv[:, X_WIDTH:].astype(bf16)


def _memkv(mem2d, g, wkv, kg, B, Nm):
    return pl.pallas_call(
        _memkv_kernel,
        grid=(B,),
        in_specs=[pl.BlockSpec((Nm, D_MODEL), lambda b: (b, 0)),
                  pl.BlockSpec((1, D_MODEL), lambda b: (0, 0)),
                  pl.BlockSpec((D_MODEL, 2 * X_WIDTH), lambda b: (0, 0)),
                  pl.BlockSpec((1, X_HD), lambda b: (0, 0))],
        out_specs=[pl.BlockSpec((Nm, X_WIDTH), lambda b: (b, 0)),
                   pl.BlockSpec((Nm, X_WIDTH), lambda b: (b, 0))],
        out_shape=[jax.ShapeDtypeStruct((B * Nm, X_WIDTH), bf16)] * 2,
        compiler_params=pltpu.CompilerParams(
            dimension_semantics=("arbitrary",), vmem_limit_bytes=VMEM_LIMIT),
        name="mem_kv",
    )(mem2d, g, wkv, kg)


def _back_kernel(x_ref, ym_ref, ya_ref, wout_ref, gx_ref, wqx_ref, xqg_ref, kx_ref, vx_ref,
                 wox_ref, gf_ref, w1_ref, w2_ref, out_ref, x2_ref):
    @pl.when(pl.program_id(0) == 0)
    def _():
        x2_ref[...] = jnp.zeros(x2_ref.shape, f32)

    x2p = x2_ref[...]
    hf = _rms(x2p, gf_ref[...]).astype(bf16)
    mlp = {"acc": x2p, "a": None}

    def mlp_step(k):
        c, second = divmod(k, 2)
        sl = slice(c * FF_CHUNK, (c + 1) * FF_CHUNK)
        if not second:
            a = jnp.maximum(_dot(hf, w1_ref[:, sl]), 0.0)
            mlp["a"] = (a * a).astype(bf16)
        else:
            mlp["acc"] = mlp["acc"] + _dot(mlp["a"], w2_ref[sl, :])

    nsteps = 2 * (D_FF // FF_CHUNK)
    x1 = (x_ref[...] + _dot(ym_ref[...], wout_ref[0:M_WIDTH, :])
          + _dot(ya_ref[...], wout_ref[M_WIDTH:2 * M_WIDTH, :]))
    mlp_step(0)
    hn = _rms(x1, gx_ref[...]).astype(bf16)
    qx = _dot(hn, wqx_ref[...])
    mlp_step(1)
    ps = []
    for hh in range(X_HEADS):
        sl = slice(hh * X_HD, (hh + 1) * X_HD)
        qh = (_rms(qx[:, sl], xqg_ref[...]) * (X_HD ** -0.5)).astype(bf16)
        s = _dot_nt(qh, kx_ref[:, sl])
        ps.append(jnp.exp(s - jnp.max(s, axis=-1, keepdims=True)))
    mlp_step(2)
    heads = []
    for hh in range(X_HEADS):
        sl = slice(hh * X_HD, (hh + 1) * X_HD)
        o = _dot(ps[hh].astype(bf16), vx_ref[:, sl]) / jnp.sum(ps[hh], axis=-1, keepdims=True)
        heads.append(o.astype(bf16))
    mlp_step(3)
    x2 = x1 + _dot(jnp.concatenate(heads, axis=-1), wox_ref[...])
    for k in range(4, nsteps):
        mlp_step(k)
    out_ref[...] = mlp["acc"]
    x2_ref[...] = x2


def _back(x, ym, ya, kx, vx, w, B, S, Nm, tm):
    T = B * S
    G = T // tm
    row = lambda i: (jnp.minimum(i, G - 1), 0)
    lag = lambda i: (jnp.maximum(i - 1, 0), 0)
    c2 = lambda i: (0, 0)
    per_b = lambda i: ((jnp.minimum(i, G - 1) * tm) // S, 0)

    def const(a):
        return pl.BlockSpec(a.shape, c2, pipeline_mode=pl.Buffered(1))

    return pl.pallas_call(
        _back_kernel,
        grid=(G + 1,),
        in_specs=[pl.BlockSpec((tm, D_MODEL), row),
                  pl.BlockSpec((tm, M_WIDTH), row),
                  pl.BlockSpec((tm, M_WIDTH), row),
                  const(w["wout"]), const(w["gx"]), const(w["wqx"]), const(w["xqg"]),
                  pl.BlockSpec((Nm, X_WIDTH), per_b),
                  pl.BlockSpec((Nm, X_WIDTH), per_b),
                  const(w["wox"]), const(w["gf"]), const(w["w1"]), const(w["w2"])],
        out_specs=pl.BlockSpec((tm, D_MODEL), lag),
        out_shape=jax.ShapeDtypeStruct((T, D_MODEL), f32),
        scratch_shapes=[pltpu.VMEM((tm, D_MODEL), f32)],
        compiler_params=pltpu.CompilerParams(
            dimension_semantics=("arbitrary",), vmem_limit_bytes=VMEM_LIMIT),
        name="back",
    )(x, ym, ya, w["wout"], w["gx"], w["wqx"], w["xqg"], kx, vx, w["wox"], w["gf"], w["w1"], w["w2"])


def _rope_pad(a):
    z = jnp.zeros(a.shape[:-1] + (ROPE_DIM // 2,), a.dtype)
    return jnp.concatenate([a[..., :32], z, a[..., 32:], z], axis=-1)


def _block_diag_heads(w):
    H = w.shape[0]
    wp = jnp.pad(w, ((0, 0), (0, 0), (0, M_DKP - M_DK)))
    eye = jnp.eye(H, dtype=w.dtype)
    return (wp[:, :, None, :] * eye[:, None, :, None]).reshape(H * M_DV, H * M_DKP)


def _layer_weights(l, norm_mix_g, w_in, conv_w, conv_b, wq_m, wk_m, b_igate, b_fgate, m_out_g,
                   cq_norm_g, ckv_norm_g, w_uq, w_ukv, qk_norm_q, qk_norm_k, a_out_g, w_out,
                   norm_x_g, norm_mem_g, wq_x, wkv_x, xq_norm_g, xk_norm_g, wo_x,
                   norm_ffn_g, w_ff1, w_ff2):
    wi = w_in[l].astype(bf16)
    o_v, o_o, o_g = M_WIDTH, 2 * M_WIDTH, 3 * M_WIDTH
    o_f = o_g + M_HEADS
    o_cq = o_f + M_HEADS
    o_ckv = o_cq + Q_LORA
    o_kr = o_ckv + KV_LORA
    wnat = jnp.concatenate([wi[:, :o_v], wi[:, o_o:o_g], wi[:, o_cq:o_kr], _rope_pad(wi[:, o_kr:])], axis=1)
    zg = jnp.zeros((8 - M_HEADS, D_MODEL), wi.dtype)
    wtr = jnp.concatenate([wi[:, o_v:o_o].T, wi[:, o_g:o_f].T, zg, wi[:, o_f:o_cq].T, zg], axis=0)
    z4 = jnp.zeros((8 - M_HEADS,), f32)
    gbias = jnp.concatenate([b_igate[l], z4, b_fgate[l], z4])[:, None]
    wuq = w_uq[l].astype(bf16).reshape(Q_LORA, A_HEADS, A_QK)
    wuq = jnp.concatenate([wuq[..., :NOPE_DIM], _rope_pad(wuq[..., NOPE_DIM:])], axis=-1)
    wukv = w_ukv[l].astype(bf16).reshape(KV_LORA, A_HEADS, NOPE_DIM + A_DV)
    return {
        "gmix": norm_mix_g[l][None, :],
        "wnat": wnat,
        "wtr": wtr,
        "convw": conv_w[l],
        "convb": conv_b[l][None, :],
        "wqt": _block_diag_heads((wq_m[l] * (M_DK ** -0.5)).astype(bf16)).T,
        "wk": _block_diag_heads(wk_m[l].astype(bf16)),
        "gbias": gbias,
        "mog": m_out_g[l].reshape(1, M_WIDTH),
        "cqg": cq_norm_g[l][None, :],
        "ckvg": ckv_norm_g[l][None, :],
        "wuq": wuq.reshape(Q_LORA, A_HEADS * QK_PAD),
        "wukvk": wukv[..., :NOPE_DIM].reshape(KV_LORA, A_HEADS * NOPE_DIM),
        "wukvvt": wukv[..., NOPE_DIM:].reshape(KV_LORA, A_HEADS * A_DV).T,
        "gq": jnp.concatenate([qk_norm_q[l][:NOPE_DIM], _rope_pad(qk_norm_q[l][NOPE_DIM:])])[None, :],
        "gk": jnp.concatenate([qk_norm_k[l][:NOPE_DIM], _rope_pad(qk_norm_k[l][NOPE_DIM:])])[None, :],
        "aog": a_out_g[l].reshape(A_HEADS * A_DV, 1),
        "wout": w_out[l].astype(bf16),
        "gx": norm_x_g[l][None, :],
        "gmem": norm_mem_g[l][None, :],
        "wqx": wq_x[l].astype(bf16),
        "wkvx": wkv_x[l].astype(bf16),
        "xqg": xq_norm_g[l][None, :],
        "xkg": xk_norm_g[l][None, :],
        "wox": wo_x[l].astype(bf16),
        "gf": norm_ffn_g[l][None, :],
        "w1": w_ff1[l].astype(bf16),
        "w2": w_ff2[l].astype(bf16),
    }


def kernel(x, mem, positions, norm_mix_g, w_in, conv_w, conv_b, wq_m, wk_m, b_igate, b_fgate, m_out_g, cq_norm_g, ckv_norm_g, w_uq, w_ukv, qk_norm_q, qk_norm_k, a_out_g, w_out, norm_x_g, norm_mem_g, wq_x, wkv_x, xq_norm_g, xk_norm_g, wo_x, norm_ffn_g, w_ff1, w_ff2):
    B, S, D = x.shape
    Nm = mem.shape[1]
    depth = w_in.shape[0]
    assert D == D_MODEL and S % (2 * SUB) == 0
    tm_back = SUB
    params = (norm_mix_g, w_in, conv_w, conv_b, wq_m, wk_m, b_igate, b_fgate, m_out_g,
              cq_norm_g, ckv_norm_g, w_uq, w_ukv, qk_norm_q, qk_norm_k, a_out_g, w_out,
              norm_x_g, norm_mem_g, wq_x, wkv_x, xq_norm_g, xk_norm_g, wo_x,
              norm_ffn_g, w_ff1, w_ff2)

    cos, sin = _rope_tables(positions)
    xt = x.reshape(B * S, D)
    mem2d = mem.reshape(B * Nm, D)
    for l in range(depth):
        w = _layer_weights(l, *params)
        ym, q, k, vt = _front(xt, cos, sin, w, B, S)
        ya = _attention(q, k, vt, w["aog"], B, S, ATT_T)
        kx, vx = _memkv(mem2d, w["gmem"], w["wkvx"], w["xkg"], B, Nm)
        xt = _back(xt, ym, ya, kx, vx, w, B, S, Nm, tm_back)
    return xt.reshape(B, S, D)
```
